```python
import math
import jax, jax.numpy as jnp
from jax import lax
import numpy as np

D_MODEL = 1024
BATCH = 2
SEQ = 8192
DEPTH = 2
DEC_BATCH = 32
DEC_SEQ = 1
PAST_LEN = 8192
PAGE_SIZE = 128

HEAD_DIM = 128
N_HEADS = D_MODEL // HEAD_DIM
D_CONV = D_MODEL
CONV_W = 3
MOBA_BLOCK = 256
MOBA_TOPK = 3
Q_CHUNK = 32
N_GROUPS = 4
EXPERTS_PER_GROUP = 8
N_EXPERTS = N_GROUPS * EXPERTS_PER_GROUP
EXPERT_TOPK = 2
D_EXPERT = D_MODEL // 2
MOE_BLOCK = 128
N_A = DEPTH // 2
N_B = DEPTH - N_A
EPS = 1e-6

kernel_name = "yoco_shortconv_moba_hmoe_step"


def rms_norm(x, g):
    xf = x.astype(jnp.float32)
    y = xf * lax.rsqrt(jnp.mean(xf * xf, axis=-1, keepdims=True) + EPS)
    return (y * g.astype(jnp.float32)).astype(x.dtype)


def short_conv(xn, w_in, w_conv, w_out, prefix):
    T = xn.shape[1]
    bg, cg, v = jnp.split(xn @ w_in, 3, axis=-1)
    up = jnp.concatenate([prefix.astype(v.dtype), cg * v], axis=1)
    y = sum(w_conv[j] * up[:, j:j + T] for j in range(CONV_W))
    return (bg * y) @ w_out, up[:, T:]


def to_blocks(k, v):
    B, L, H, Dh = k.shape
    nb = max(-(-L // MOBA_BLOCK), MOBA_TOPK)
    pad = nb * MOBA_BLOCK - L
    kb = jnp.pad(k, ((0, 0), (0, pad), (0, 0), (0, 0))).reshape(B, nb, MOBA_BLOCK, H, Dh)
    vb = jnp.pad(v, ((0, 0), (0, pad), (0, 0), (0, 0))).reshape(B, nb, MOBA_BLOCK, H, Dh)
    km = jnp.mean(kb.astype(jnp.float32), axis=2).astype(k.dtype)
    return kb, vb, km


def moba_attend(q, kb, vb, km, q_pos):
    B, Tq, H, Dh = q.shape
    nb = kb.shape[1]
    own = q_pos // MOBA_BLOCK
    gate = jnp.einsum('bqhd,bnhd->bhqn', q, km, preferred_element_type=jnp.float32)
    past_ok = jnp.arange(nb)[None, :] < own[:, None]
    gate = jnp.where(past_ok, gate, -jnp.inf)
    _, top = lax.top_k(gate, MOBA_TOPK)
    own_b = jnp.broadcast_to(own[None, None, :, None], (B, H, Tq, 1)).astype(top.dtype)
    sel = jnp.concatenate([top, own_b], axis=-1)
    is_own = jnp.arange(MOBA_TOPK + 1) == MOBA_TOPK
    sel_ok = (sel < own[None, None, :, None]) | is_own
    bi = jnp.arange(B)[:, None, None, None]
    hi = jnp.arange(H)[None, :, None, None]
    kg = kb[bi, sel, :, hi]
    vg = vb[bi, sel, :, hi]
    key_pos = sel[..., None] * MOBA_BLOCK + jnp.arange(MOBA_BLOCK)
    mask = sel_ok[..., None] & (key_pos <= q_pos[None, None, :, None, None])
    logits = jnp.einsum('bqhd,bhqsjd->bhqsj', q, kg,
                        preferred_element_type=jnp.float32) * (HEAD_DIM ** -0.5)
    logits = jnp.where(mask, logits, -jnp.inf).reshape(B, H, Tq, -1)
    p = jax.nn.softmax(logits, axis=-1).reshape(mask.shape).astype(vg.dtype)
    return jnp.einsum('bhqsj,bhqsjd->bqhd', p, vg)


def moba_chunked(q, kb, vb, km, q_pos):
    B, T, H, Dh = q.shape
    qc = Q_CHUNK if T % Q_CHUNK == 0 else T
    n = T // qc
    qs = q.reshape(B, n, qc, H, Dh).transpose(1, 0, 2, 3, 4)
    ps = q_pos.reshape(n, qc)
    out = lax.map(lambda a: moba_attend(a[0], kb, vb, km, a[1]), (qs, ps))
    return out.transpose(1, 0, 2, 3, 4).reshape(B, T, H, Dh)


def grouped_experts(x, eid, ew, w_gate, w_up, w_down):
    T, D = x.shape
    K = eid.shape[1]
    E = w_gate.shape[0]
    A = T * K
    if A >= E * MOE_BLOCK:
        blk = MOE_BLOCK
    else:
        blk = max(8, 1 << int(math.ceil(math.log2(max(1, -(-A // E))))))
    e_flat = eid.reshape(A)
    w_flat = ew.reshape(A)
    tok = jnp.arange(A, dtype=jnp.int32) // K
    counts = jnp.bincount(e_flat, length=E)
    padded = ((counts + blk - 1) // blk) * blk
    pad_end = jnp.cumsum(padded)
    pad_start = pad_end - padded
    raw_start = jnp.cumsum(counts) - counts
    order = jnp.argsort(e_flat)
    e_sorted = e_flat[order]
    dest = pad_start[e_sorted] + (jnp.arange(A) - raw_start[e_sorted])
    n_blk = -(-A // blk) + E
    P = n_blk * blk
    row_tok = jnp.full((P,), T, jnp.int32).at[dest].set(tok[order])
    row_w = jnp.zeros((P,), x.dtype).at[dest].set(w_flat[order].astype(x.dtype))
    blk_expert = jnp.clip(jnp.searchsorted(pad_end, jnp.arange(n_blk) * blk, side='right'), 0, E - 1)
    x_pad = jnp.concatenate([x, jnp.zeros((1, D), x.dtype)], axis=0)
    xs = x_pad[row_tok].reshape(n_blk, blk, D)

    def run(args):
        xb, e = args
        h = jax.nn.silu(xb @ w_gate[e]) * (xb @ w_up[e])
        return h @ w_down[e]

    ys = lax.map(run, (xs, blk_expert)).reshape(P, D) * row_w[:, None]
    return jax.ops.segment_sum(ys, row_tok, num_segments=T + 1)[:T]


def hmoe(x, w_rg, b_rg, w_re, b_re, w_gate, w_up, w_down):
    Bx, T, D = x.shape
    xf = x.reshape(-1, D)
    g_logits = (xf @ w_rg).astype(jnp.float32) + b_rg.astype(jnp.float32)
    g_prob = jax.nn.softmax(g_logits, axis=-1)
    g_sel = jnp.argmax(g_logits, axis=-1).astype(jnp.int32)
    p_group = jnp.take_along_axis(g_prob, g_sel[:, None], axis=-1)
    e_logits = ((xf @ w_re).astype(jnp.float32) + b_re.astype(jnp.float32)).reshape(
        -1, N_GROUPS, EXPERTS_PER_GROUP)
    e_in = jnp.take_along_axis(e_logits, g_sel[:, None, None], axis=1)[:, 0]
    top_p, top_i = lax.top_k(jax.nn.softmax(e_in, axis=-1), EXPERT_TOPK)
    weights = p_group * top_p / jnp.sum(top_p, axis=-1, keepdims=True)
    eid = g_sel[:, None] * EXPERTS_PER_GROUP + top_i
    out = grouped_experts(xf, eid, weights, w_gate, w_up, w_down)
    return out.reshape(Bx, T, D)


def setup_inputs(seed: int = 0) -> dict:
    key = jax.random.key(seed)
    ks = jax.random.split(key, 28)
    n_pages = PAST_LEN // PAGE_SIZE
    n_used = DEC_BATCH * n_pages
    n_phys = (n_used * 5) // 4

    def nrm(k, shape, scale=1.0):
        return jax.random.normal(k, shape, jnp.float32) * scale

    page_table = jax.random.permutation(ks[5], n_phys)[:n_used].reshape(DEC_BATCH, n_pages).astype(jnp.int32)
    HD = N_HEADS * HEAD_DIM
    return {
        "x_prompt": nrm(ks[0], (BATCH, SEQ, D_MODEL)),
        "x_sample": nrm(ks[1], (DEC_BATCH, DEC_SEQ, D_MODEL)),
        "state_conv": nrm(ks[2], (N_A, DEC_BATCH, CONV_W - 1, D_CONV)),
        "cache_k": nrm(ks[3], (n_phys, PAGE_SIZE, N_HEADS, HEAD_DIM)),
        "cache_v": nrm(ks[4], (n_phys, PAGE_SIZE, N_HEADS, HEAD_DIM)),
        "page_table": page_table,
        "norm_mix": 1.0 + nrm(ks[6], (DEPTH, D_MODEL), 0.02),
        "norm_ffn": 1.0 + nrm(ks[7], (DEPTH, D_MODEL), 0.02),
        "norm_kv": 1.0 + nrm(ks[8], (D_MODEL,), 0.02),
        "w_conv_in": nrm(ks[9], (N_A, D_MODEL, 3 * D_CONV), D_MODEL ** -0.5),
        "w_conv": nrm(ks[10], (N_A, CONV_W, D_CONV), CONV_W ** -0.5),
        "w_conv_out": nrm(ks[11], (N_A, D_CONV, D_MODEL), D_CONV ** -0.5),
        "w_kv": nrm(ks[12], (D_MODEL, 2 * HD), D_MODEL ** -0.5),
        "k_norm": 1.0 + nrm(ks[13], (HEAD_DIM,), 0.02),
        "w_q": nrm(ks[14], (N_B, D_MODEL, HD), D_MODEL ** -0.5),
        "q_norm": 1.0 + nrm(ks[15], (N_B, HEAD_DIM), 0.02),
        "w_o": nrm(ks[16], (N_B, HD, D_MODEL), HD ** -0.5),
        "w_router_group": nrm(ks[17], (DEPTH, D_MODEL, N_GROUPS), D_MODEL ** -0.5),
        "b_router_group": nrm(ks[18], (DEPTH, N_GROUPS), 0.01),
        "w_router_expert": nrm(ks[19], (DEPTH, D_MODEL, N_EXPERTS), D_MODEL ** -0.5),
        "b_router_expert": nrm(ks[20], (DEPTH, N_EXPERTS), 0.01),
        "w_expert_gate": nrm(ks[21], (DEPTH, N_EXPERTS, D_MODEL, D_EXPERT), D_MODEL ** -0.5),
        "w_expert_up": nrm(ks[22], (DEPTH, N_EXPERTS, D_MODEL, D_EXPERT), D_MODEL ** -0.5),
        "w_expert_down": nrm(ks[23], (DEPTH, N_EXPERTS, D_EXPERT, D_MODEL), D_EXPERT ** -0.5),
    }


def reference(x_prompt, x_sample, state_conv, cache_k, cache_v, page_table, norm_mix, norm_ffn,
              norm_kv, w_conv_in, w_conv, w_conv_out, w_kv, k_norm, w_q, q_norm, w_o,
              w_router_group, b_router_group, w_router_expert, b_router_expert,
              w_expert_gate, w_expert_up, w_expert_down):

    def shared_kv(h):
        B, T = h.shape[:2]
        k, v = jnp.split(rms_norm(h, norm_kv) @ w_kv, 2, axis=-1)
        k = rms_norm(k.reshape(B, T, N_HEADS, HEAD_DIM), k_norm)
        return k, v.reshape(B, T, N_HEADS, HEAD_DIM)

    def trunk(x, conv_prefix, make_ctx, q_pos):
        new_conv = []
        k_new = v_new = ctx = None
        for l in range(DEPTH):
            h = rms_norm(x, norm_mix[l])
            if l < N_A:
                y, st = short_conv(h, w_conv_in[l], w_conv[l], w_conv_out[l], conv_prefix[l])
                new_conv.append(st)
            else:
                j = l - N_A
                B, T = h.shape[:2]
                q = rms_norm((h @ w_q[j]).reshape(B, T, N_HEADS, HEAD_DIM), q_norm[j])
                y = moba_chunked(q, ctx[0], ctx[1], ctx[2], q_pos).reshape(B, T, -1) @ w_o[j]
            x = x + y
            x = x + hmoe(rms_norm(x, norm_ffn[l]), w_router_group[l], b_router_group[l],
                         w_router_expert[l], b_router_expert[l],
                         w_expert_gate[l], w_expert_up[l], w_expert_down[l])
            if l == N_A - 1:
                k_new, v_new = shared_kv(x)
                ctx = make_ctx(k_new, v_new)
        return x, jnp.stack(new_conv), k_new, v_new

    Bp, Tp = x_prompt.shape[:2]
    conv_zero = jnp.zeros((N_A, Bp, CONV_W - 1, D_CONV), x_prompt.dtype)
    y_prompt, conv_prompt, k_prompt, v_prompt = trunk(
        x_prompt, conv_zero, to_blocks, jnp.arange(Tp, dtype=jnp.int32))

    Bs, Ts = x_sample.shape[:2]
    past_len = page_table.shape[1] * cache_k.shape[1]
    k_past = cache_k[page_table].reshape(Bs, past_len, N_HEADS, HEAD_DIM)
    v_past = cache_v[page_table].reshape(Bs, past_len, N_HEADS, HEAD_DIM)

    def sample_ctx(k_new, v_new):
        return to_blocks(jnp.concatenate([k_past, k_new.astype(k_past.dtype)], axis=1),
                         jnp.concatenate([v_past, v_new.astype(v_past.dtype)], axis=1))

    y_sample, conv_sample, k_sample, v_sample = trunk(
        x_sample, state_conv, sample_ctx, past_len + jnp.arange(Ts, dtype=jnp.int32))

    return (y_prompt, y_sample, conv_prompt, k_prompt, v_prompt, conv_sample, k_sample, v_sample)
```

```python
import functools

import jax
import jax.numpy as jnp
from jax import lax
from jax.experimental import pallas as pl
from jax.experimental.pallas import tpu as pltpu

F32 = jnp.float32
BF16 = jnp.bfloat16
I32 = jnp.int32
HI = lax.Precision.HIGHEST

EPS = 1e-6
HEAD_DIM = 128
MOBA_BLOCK = 256
MOBA_TOPK = 3
N_GROUPS = 4
EXPERTS_PER_GROUP = 8
EXPERT_TOPK = 2
CONV_W = 3

LANE = 128
SUBLANE = 8
VMEM_LIMIT = 48 * 1024 * 1024
NEG = -1e30
BIG_IDX = 1 << 20


def _cparams(*sem):
    return pltpu.CompilerParams(dimension_semantics=sem, vmem_limit_bytes=VMEM_LIMIT)


def _rms(x, g):
    ms = jnp.mean(x * x, axis=-1, keepdims=True)
    return x * lax.rsqrt(ms + EPS) * g


def _dot(a, b, precise):
    if precise:
        return jnp.dot(a, b, precision=HI, preferred_element_type=F32)
    return jnp.dot(a.astype(BF16), b.astype(BF16), preferred_element_type=F32)


def _mixer_prompt_kernel(x_ref, g_ref, win_ref, wc_ref, wout_ref, o_ref, st_ref, buf_ref, *, tm, d):
    @pl.when(pl.program_id(1) == 0)
    def _():
        buf_ref[0:SUBLANE, :] = jnp.zeros((SUBLANE, d), F32)

    x = x_ref[...]
    h = _rms(x, g_ref[...])
    u = _dot(h, win_ref[...], False)
    bg = u[:, :d]
    up = u[:, d:2 * d] * u[:, 2 * d:]
    buf_ref[SUBLANE:SUBLANE + tm, :] = up
    up1 = buf_ref[SUBLANE - 1:SUBLANE - 1 + tm, :]
    up2 = buf_ref[SUBLANE - 2:SUBLANE - 2 + tm, :]
    wc = wc_ref[...]
    y = wc[0:1] * up2 + wc[1:2] * up1 + wc[2:3] * up
    o_ref[...] = x + _dot(bg * y, wout_ref[...], False)
    last2 = buf_ref[tm + SUBLANE - 2:tm + SUBLANE, :]
    st_ref[...] = last2
    buf_ref[SUBLANE - 2:SUBLANE, :] = last2


def _mixer_prompt(x, g, w_in, w_c, w_out, tm=256):
    b, t, d = x.shape
    kern = functools.partial(_mixer_prompt_kernel, tm=tm, d=d)
    return pl.pallas_call(
        kern,
        grid=(b, t // tm),
        in_specs=[
            pl.BlockSpec((None, tm, d), lambda i, j: (i, j, 0)),
            pl.BlockSpec((1, d), lambda i, j: (0, 0)),
            pl.BlockSpec((d, 3 * d), lambda i, j: (0, 0)),
            pl.BlockSpec((CONV_W, d), lambda i, j: (0, 0)),
            pl.BlockSpec((d, d), lambda i, j: (0, 0)),
        ],
        out_specs=[
            pl.BlockSpec((None, tm, d), lambda i, j: (i, j, 0)),
            pl.BlockSpec((None, CONV_W - 1, d), lambda i, j: (i, 0, 0)),
        ],
        out_shape=[
            jax.ShapeDtypeStruct((b, t, d), F32),
            jax.ShapeDtypeStruct((b, CONV_W - 1, d), F32),
        ],
        scratch_shapes=[pltpu.VMEM((tm + SUBLANE, d), F32)],
        compiler_params=_cparams("arbitrary", "arbitrary"),
        name="mixer_prompt",
    )(x, g, w_in.astype(BF16), w_c, w_out.astype(BF16))


def _mixer_sample_kernel(x_ref, p0_ref, p1_ref, g_ref, win_ref, wc_ref, wout_ref, o_ref, up_ref, *, d):
    x = x_ref[...]
    h = _rms(x, g_ref[...])
    u = _dot(h, win_ref[...], True)
    bg = u[:, :d]
    up = u[:, d:2 * d] * u[:, 2 * d:]
    wc = wc_ref[...]
    y = wc[0:1] * p0_ref[...] + wc[1:2] * p1_ref[...] + wc[2:3] * up
    o_ref[...] = x + _dot(bg * y, wout_ref[...], True)
    up_ref[...] = up


def _mixer_sample(x, p0, p1, g, w_in, w_c, w_out):
    n, d = x.shape
    return pl.pallas_call(
        functools.partial(_mixer_sample_kernel, d=d),
        out_shape=[jax.ShapeDtypeStruct((n, d), F32), jax.ShapeDtypeStruct((n, d), F32)],
        compiler_params=pltpu.CompilerParams(vmem_limit_bytes=VMEM_LIMIT),
        name="mixer_sample",
    )(x, p0, p1, g, w_in, w_c, w_out)


def _router_kernel(x_ref, g_ref, wr_ref, br_ref, xn_ref, info_ref):
    xn = _rms(x_ref[...], g_ref[...])
    xn_ref[...] = xn.astype(BF16)
    lg = _dot(xn, wr_ref[...], True) + br_ref[...]
    lane = lax.broadcasted_iota(I32, lg.shape, 1)
    ninf = jnp.float32(-jnp.inf)
    is_g = lane < N_GROUPS
    gl = jnp.where(is_g, lg, ninf)
    gmax = jnp.max(gl, axis=-1, keepdims=True)
    gsel = jnp.min(jnp.where(gl == gmax, lane, BIG_IDX), axis=-1, keepdims=True)
    gsum = jnp.sum(jnp.where(is_g, jnp.exp(gl - gmax), 0.0), axis=-1, keepdims=True)
    p_group = 1.0 / gsum
    lo = N_GROUPS + gsel * EXPERTS_PER_GROUP
    in_grp = (lane >= lo) & (lane < lo + EXPERTS_PER_GROUP)
    el = jnp.where(in_grp, lg, ninf)
    e1 = jnp.max(el, axis=-1, keepdims=True)
    i1 = jnp.min(jnp.where(el == e1, lane, BIG_IDX), axis=-1, keepdims=True)
    el2 = jnp.where(lane == i1, ninf, el)
    e2 = jnp.max(el2, axis=-1, keepdims=True)
    i2 = jnp.min(jnp.where(el2 == e2, lane, BIG_IDX), axis=-1, keepdims=True)
    r = jnp.exp(e2 - e1)
    w1 = p_group / (1.0 + r)
    w2 = w1 * r
    info = jnp.where(lane == 0, (i1 - N_GROUPS).astype(F32),
                     jnp.where(lane == 1, (i2 - N_GROUPS).astype(F32),
                               jnp.where(lane == 2, w1, jnp.where(lane == 3, w2, 0.0))))
    info_ref[...] = info


def _router(x, g, wr, br, tm):
    n, d = x.shape
    return pl.pallas_call(
        _router_kernel,
        grid=(n // tm,),
        in_specs=[
            pl.BlockSpec((tm, d), lambda i: (i, 0)),
            pl.BlockSpec((1, d), lambda i: (0, 0)),
            pl.BlockSpec((d, LANE), lambda i: (0, 0)),
            pl.BlockSpec((1, LANE), lambda i: (0, 0)),
        ],
        out_specs=[
            pl.BlockSpec((tm, d), lambda i: (i, 0)),
            pl.BlockSpec((tm, LANE), lambda i: (i, 0)),
        ],
        out_shape=[jax.ShapeDtypeStruct((n, d), BF16), jax.ShapeDtypeStruct((n, LANE), F32)],
        compiler_params=_cparams("arbitrary"),
        name="router",
    )(x, g, wr, br)


def _experts_kernel(be_ref, nu_ref, xs_ref, wg_ref, wu_ref, wd_ref, o_ref, wgb, wub, wdb):
    i = pl.program_id(0)
    e = be_ref[i]
    prev = be_ref[jnp.maximum(i - 1, 0)]

    @pl.when((i == 0) | (e != prev))
    def _():
        wgb[...] = wg_ref[...].astype(BF16)
        wub[...] = wu_ref[...].astype(BF16)
        wdb[...] = wd_ref[...].astype(BF16)

    @pl.when(i < nu_ref[0])
    def _():
        xb = xs_ref[...]
        h1 = jnp.dot(xb, wgb[...], preferred_element_type=F32)
        h2 = jnp.dot(xb, wub[...], preferred_element_type=F32)
        h = h1 * jax.nn.sigmoid(h1) * h2
        o_ref[...] = jnp.dot(h.astype(BF16), wdb[...], preferred_element_type=F32)

    @pl.when(i >= nu_ref[0])
    def _():
        o_ref[...] = jnp.zeros(o_ref.shape, F32)


def _experts(xs, blk_expert, n_used, wg, wu, wd, layer, blk):
    p, d = xs.shape
    de = wg.shape[-1]
    n_blk = p // blk
    grid_spec = pltpu.PrefetchScalarGridSpec(
        num_scalar_prefetch=2,
        grid=(n_blk,),
        in_specs=[
            pl.BlockSpec((blk, d), lambda i, be, nu: (jnp.minimum(i, nu[0] - 1), 0)),
            pl.BlockSpec((None, None, d, de), lambda i, be, nu: (layer, be[i], 0, 0)),
            pl.BlockSpec((None, None, d, de), lambda i, be, nu: (layer, be[i], 0, 0)),
            pl.BlockSpec((None, None, de, d), lambda i, be, nu: (layer, be[i], 0, 0)),
        ],
        out_specs=pl.BlockSpec((blk, d), lambda i, be, nu: (i, 0)),
        scratch_shapes=[pltpu.VMEM((d, de), BF16), pltpu.VMEM((d, de), BF16), pltpu.VMEM((de, d), BF16)],
    )
    return pl.pallas_call(
        _experts_kernel,
        grid_spec=grid_spec,
        out_shape=jax.ShapeDtypeStruct((p, d), F32),
        compiler_params=_cparams("arbitrary"),
        name="experts",
    )(blk_expert, n_used, xs, wg, wu, wd)


def _moe(x, layer, g, w_rg, b_rg, w_re, b_re, wg, wu, wd, *, tm, blk):
    n, d = x.shape
    n_exp = wg.shape[1]
    pad = LANE - N_GROUPS - n_exp
    wr = jnp.concatenate([w_rg, w_re, jnp.zeros((d, pad), F32)], axis=1)
    br = jnp.concatenate([b_rg, b_re, jnp.zeros((pad,), F32)])[None, :]
    xn, info = _router(x, g, wr, br, tm)
    eid = info[:, 0:EXPERT_TOPK].astype(I32)
    wts = info[:, EXPERT_TOPK:2 * EXPERT_TOPK]

    a = n * EXPERT_TOPK
    e_flat = eid.reshape(a)
    onehot = (e_flat[:, None] == jnp.arange(n_exp, dtype=I32)[None, :]).astype(I32)
    csum = jnp.cumsum(onehot, axis=0)
    counts = csum[-1]
    rank = jnp.sum(onehot * csum, axis=1) - 1
    padded = ((counts + blk - 1) // blk) * blk
    pad_end = jnp.cumsum(padded)
    pad_start = pad_end - padded
    dest = pad_start[e_flat] + rank
    n_blk = -(-a // blk) + n_exp
    n_used = (pad_end[-1] // blk).astype(I32)
    blk_ids = jnp.minimum(jnp.arange(n_blk, dtype=I32), n_used - 1)
    blk_expert = jnp.clip(jnp.searchsorted(pad_end, blk_ids * blk, side='right'), 0, n_exp - 1).astype(I32)
    tok = jnp.arange(a, dtype=I32) // EXPERT_TOPK
    row_tok = jnp.zeros((n_blk * blk,), I32).at[dest].set(tok)
    xs = jnp.take(xn, row_tok, axis=0)
    ys = _experts(xs, blk_expert, n_used.reshape(1), wg, wu, wd, layer, blk)
    dest2 = dest.reshape(n, EXPERT_TOPK)
    out = x
    for k in range(EXPERT_TOPK):
        out = out + wts[:, k:k + 1] * jnp.take(ys, dest2[:, k], axis=0)
    return out


def _head_rms(x, g):
    return x * lax.rsqrt(jnp.mean(x * x, axis=-1, keepdims=True) + EPS) * g


def _kvq_prompt_kernel(x_ref, gkv_ref, gq_ref, wkv_ref, wq_ref, kn_ref, qn_ref,
                       k_ref, v_ref, kb_ref, vt_ref, qt_ref, bias_ref, km_ref, *, n_heads, nb):
    i = pl.program_id(1)
    d = n_heads * HEAD_DIM

    @pl.when(i == 0)
    def _():
        km_ref[...] = jnp.zeros(km_ref.shape, F32)

    x = x_ref[...]
    kv = _dot(_rms(x, gkv_ref[...]), wkv_ref[...], False)
    q = _dot(_rms(x, gq_ref[...]), wq_ref[...], False)
    v_ref[...] = kv[:, d:]
    tq = x.shape[0]
    blk_iota = lax.broadcasted_iota(I32, (nb, tq), 0)
    row_iota = lax.broadcasted_iota(I32, (nb, HEAD_DIM), 0)
    ninf = jnp.float32(-jnp.inf)
    for h in range(n_heads):
        hs = slice(h * HEAD_DIM, (h + 1) * HEAD_DIM)
        kh = _head_rms(kv[:, hs], kn_ref[...])
        k_ref[:, hs] = kh
        kb_ref[h] = kh.astype(BF16)
        vt_ref[h] = kv[:, d + h * HEAD_DIM:d + (h + 1) * HEAD_DIM].T.astype(BF16)
        qt = _head_rms(q[:, hs], qn_ref[...]).T
        qt_ref[h] = (qt * (HEAD_DIM ** -0.5)).astype(BF16)
        kmh = km_ref[:, hs]
        gate = jnp.dot(kmh, qt, precision=HI, preferred_element_type=F32)
        gate = jnp.where(blk_iota < i, gate, ninf)
        sel = jnp.zeros((nb, tq), jnp.bool_)
        for _ in range(MOBA_TOPK):
            m = jnp.max(gate, axis=0, keepdims=True)
            idx = jnp.min(jnp.where(gate == m, blk_iota, BIG_IDX), axis=0, keepdims=True)
            hit = blk_iota == idx
            sel = sel | (hit & (m > ninf))
            gate = jnp.where(hit, ninf, gate)
        bias_ref[h] = jnp.where(sel, 0.0, NEG)
        kmean = jnp.mean(kh, axis=0, keepdims=True)
        km_ref[:, hs] = jnp.where(row_iota == i, kmean, kmh)


def _kvq_prompt(x, gkv, gq, w_kv, w_q, kn, qn):
    b, t, d = x.shape
    n_heads = d // HEAD_DIM
    tq = MOBA_BLOCK
    nb = t // tq
    kern = functools.partial(_kvq_prompt_kernel, n_heads=n_heads, nb=nb)
    const = lambda i, j: (0, 0)
    return pl.pallas_call(
        kern,
        grid=(b, nb),
        in_specs=[
            pl.BlockSpec((None, tq, d), lambda i, j: (i, j, 0)),
            pl.BlockSpec((1, d), const),
            pl.BlockSpec((1, d), const),
            pl.BlockSpec((d, 2 * d), const),
            pl.BlockSpec((d, d), const),
            pl.BlockSpec((1, HEAD_DIM), const),
            pl.BlockSpec((1, HEAD_DIM), const),
        ],
        out_specs=[
            pl.BlockSpec((None, tq, d), lambda i, j: (i, j, 0)),
            pl.BlockSpec((None, tq, d), lambda i, j: (i, j, 0)),
            pl.BlockSpec((None, n_heads, tq, HEAD_DIM), lambda i, j: (i, 0, j, 0)),
            pl.BlockSpec((None, n_heads, None, HEAD_DIM, tq), lambda i, j: (i, 0, j, 0, 0)),
            pl.BlockSpec((None, n_heads, None, HEAD_DIM, tq), lambda i, j: (i, 0, j, 0, 0)),
            pl.BlockSpec((None, n_heads, None, nb, tq), lambda i, j: (i, 0, j, 0, 0)),
        ],
        out_shape=[
            jax.ShapeDtypeStruct((b, t, d), F32),
            jax.ShapeDtypeStruct((b, t, d), F32),
            jax.ShapeDtypeStruct((b, n_heads, t, HEAD_DIM), BF16),
            jax.ShapeDtypeStruct((b, n_heads, nb, HEAD_DIM, tq), BF16),
            jax.ShapeDtypeStruct((b, n_heads, nb, HEAD_DIM, tq), BF16),
            jax.ShapeDtypeStruct((b, n_heads, nb, nb, tq), F32),
        ],
        scratch_shapes=[pltpu.VMEM((nb, d), F32)],
        compiler_params=_cparams("arbitrary", "arbitrary"),
        name="kvq_prompt",
    )(x, gkv, gq, w_kv.astype(BF16), w_q.astype(BF16), kn, qn)


def _moba_prompt_kernel(qt_ref, bias_ref, kb_ref, vt_ref, o_ref, *, tq):
    i = pl.program_id(2)
    qt = qt_ref[...]

    start = pl.multiple_of(i * tq, tq)
    s = jnp.dot(kb_ref[pl.ds(start, tq), :], qt, preferred_element_type=F32)
    key_i = lax.broadcasted_iota(I32, s.shape, 0)
    qry_i = lax.broadcasted_iota(I32, s.shape, 1)
    s = jnp.where(key_i <= qry_i, s, NEG)
    m0 = jnp.max(s, axis=0, keepdims=True)
    p = jnp.exp(s - m0)
    l0 = jnp.sum(p, axis=0, keepdims=True)
    acc0 = jnp.dot(vt_ref[i], p.astype(BF16), preferred_element_type=F32)

    def body(j, carry):
        m, l, acc = carry
        off = pl.multiple_of(j * tq, tq)
        sj = jnp.dot(kb_ref[pl.ds(off, tq), :], qt, preferred_element_type=F32)
        sj = sj + bias_ref[pl.ds(j, 1), :]
        m_new = jnp.maximum(m, jnp.max(sj, axis=0, keepdims=True))
        alpha = jnp.exp(m - m_new)
        pj = jnp.exp(sj - m_new)
        l = alpha * l + jnp.sum(pj, axis=0, keepdims=True)
        acc = alpha * acc + jnp.dot(vt_ref[j], pj.astype(BF16), preferred_element_type=F32)
        return m_new, l, acc

    _, l, acc = lax.fori_loop(0, i, body, (m0, l0, acc0))
    o_ref[...] = (acc / l).T.astype(o_ref.dtype)


def _moba_prompt(qt, bias, kb, vt):
    b, n_heads, nb, _, tq = qt.shape
    t = nb * tq
    return pl.pallas_call(
        functools.partial(_moba_prompt_kernel, tq=tq),
        grid=(b, n_heads, nb),
        in_specs=[
            pl.BlockSpec((None, None, None, HEAD_DIM, tq), lambda bi, h, i: (bi, h, i, 0, 0)),
            pl.BlockSpec((None, None, None, nb, tq), lambda bi, h, i: (bi, h, i, 0, 0)),
            pl.BlockSpec((None, None, t, HEAD_DIM), lambda bi, h, i: (bi, h, 0, 0)),
            pl.BlockSpec((None, None, nb, HEAD_DIM, tq), lambda bi, h, i: (bi, h, 0, 0, 0)),
        ],
        out_specs=pl.BlockSpec((None, tq, HEAD_DIM), lambda bi, h, i: (bi, i, h)),
        out_shape=jax.ShapeDtypeStruct((b, t, n_heads * HEAD_DIM), BF16),
        compiler_params=_cparams("arbitrary", "arbitrary", "arbitrary"),
        name="moba_prompt",
    )(qt, bias, kb, vt)


def _proj_res_kernel(x_ref, a_ref, w_ref, o_ref, *, precise):
    o_ref[...] = x_ref[...] + _dot(a_ref[...], w_ref[...], precise)


def _proj_res(x, a, w, tm, precise):
    n, d = x.shape
    w = w if precise else w.astype(BF16)
    return pl.pallas_call(
        functools.partial(_proj_res_kernel, precise=precise),
        grid=(n // tm,),
        in_specs=[
            pl.BlockSpec((tm, d), lambda i: (i, 0)),
            pl.BlockSpec((tm, a.shape[1]), lambda i: (i, 0)),
            pl.BlockSpec(w.shape, lambda i: (0, 0)),
        ],
        out_specs=pl.BlockSpec((tm, d), lambda i: (i, 0)),
        out_shape=jax.ShapeDtypeStruct((n, d), F32),
        compiler_params=_cparams("arbitrary"),
        name="proj_res",
    )(x, a, w)


def _kvq_sample_kernel(x_ref, gkv_ref, gq_ref, wkv_ref, wq_ref, kn_ref, qn_ref, k_ref, v_ref, q_ref,
                       *, n_heads):
    d = n_heads * HEAD_DIM
    x = x_ref[...]
    kv = _dot(_rms(x, gkv_ref[...]), wkv_ref[...], True)
    q = _dot(_rms(x, gq_ref[...]), wq_ref[...], True)
    v_ref[...] = kv[:, d:]
    for h in range(n_heads):
        hs = slice(h * HEAD_DIM, (h + 1) * HEAD_DIM)
        k_ref[:, hs] = _head_rms(kv[:, hs], kn_ref[...])
        q_ref[:, hs] = _head_rms(q[:, hs], qn_ref[...])


def _kvq_sample(x, gkv, gq, w_kv, w_q, kn, qn):
    n, d = x.shape
    sds = jax.ShapeDtypeStruct((n, d), F32)
    return pl.pallas_call(
        functools.partial(_kvq_sample_kernel, n_heads=d // HEAD_DIM),
        out_shape=[sds, sds, sds],
        compiler_params=pltpu.CompilerParams(vmem_limit_bytes=VMEM_LIMIT),
        name="kvq_sample",
    )(x, gkv, gq, w_kv, w_q, kn, qn)


def _page_means_kernel(pt_ref, *refs, pages_per_step, pages_per_block):
    page_refs = refs[:pages_per_step]
    o_ref = refs[pages_per_step]
    scale = 1.0 / (pages_per_block * page_refs[0].shape[0])
    for c in range(pages_per_step // pages_per_block):
        tot = jnp.sum(page_refs[c * pages_per_block][...], axis=0)
        for r in range(1, pages_per_block):
            tot = tot + jnp.sum(page_refs[c * pages_per_block + r][...], axis=0)
        o_ref[c] = tot * scale


def _page_means(cache, page_table, pages_per_step=8):
    n_seq, n_pages = page_table.shape
    _, page, n_heads, hd = cache.shape
    ppb = MOBA_BLOCK // page
    steps = n_pages // pages_per_step
    bps = pages_per_step // ppb

    def page_spec(c):
        return pl.BlockSpec((None, page, n_heads, hd),
                            lambda s, g, pt: (pt[(s * steps + g) * pages_per_step + c], 0, 0, 0))

    grid_spec = pltpu.PrefetchScalarGridSpec(
        num_scalar_prefetch=1,
        grid=(n_seq, steps),
        in_specs=[page_spec(c) for c in range(pages_per_step)],
        out_specs=pl.BlockSpec((None, bps, n_heads, hd), lambda s, g, pt: (s, g, 0, 0)),
    )
    return pl.pallas_call(
        functools.partial(_page_means_kernel, pages_per_step=pages_per_step, pages_per_block=ppb),
        grid_spec=grid_spec,
        out_shape=jax.ShapeDtypeStruct((n_seq, n_pages // ppb, n_heads, hd), F32),
        compiler_params=_cparams("arbitrary", "arbitrary"),
        name="page_means",
    )(page_table.reshape(-1), *([cache] * pages_per_step))


def _moba_sample_kernel(pg_ref, q_ref, kn_ref, vn_ref, *refs, n_sel):
    del pg_ref
    k_refs = refs[:n_sel]
    v_refs = refs[n_sel:2 * n_sel]
    o_ref = refs[2 * n_sel]
    q = q_ref[...] * (HEAD_DIM ** -0.5)
    q8 = jnp.broadcast_to(q, (SUBLANE, HEAD_DIM))
    nt = (((1,), (1,)), ((), ()))
    s = [lax.dot_general(q8, kr[...], nt, precision=HI, preferred_element_type=F32) for kr in k_refs]
    s_new = jnp.sum(q * kn_ref[...], axis=-1, keepdims=True)
    m = s_new
    for sc in s:
        m = jnp.maximum(m, jnp.max(sc[0:1], axis=-1, keepdims=True))
    p_new = jnp.exp(s_new - m)
    l = p_new
    acc = p_new * vn_ref[...]
    for sc, vr in zip(s, v_refs):
        p = jnp.exp(sc - m)
        l = l + jnp.sum(p[0:1], axis=-1, keepdims=True)
        acc = acc + jnp.dot(p, vr[...], precision=HI, preferred_element_type=F32)[0:1]
    o_ref[...] = acc / l


def _moba_sample(q, k_new, v_new, cache_k, cache_v, sel_pages):
    sh = q.shape[0]
    n_phys, page, n_heads, hd = cache_k.shape
    n_sel = sel_pages.shape[0] // sh
    cache_k = cache_k.reshape(n_phys, page, n_heads * hd)
    cache_v = cache_v.reshape(n_phys, page, n_heads * hd)

    def slab_spec(c):
        return pl.BlockSpec((None, page, hd), lambda r, pg: (pg[r * n_sel + c], 0, r % n_heads))

    row = pl.BlockSpec((None, 1, hd), lambda r, pg: (r, 0, 0))
    grid_spec = pltpu.PrefetchScalarGridSpec(
        num_scalar_prefetch=1,
        grid=(sh,),
        in_specs=[row, row, row] + [slab_spec(c) for c in range(n_sel)] * 2,
        out_specs=row,
    )
    return pl.pallas_call(
        functools.partial(_moba_sample_kernel, n_sel=n_sel),
        grid_spec=grid_spec,
        out_shape=jax.ShapeDtypeStruct((sh, 1, hd), F32),
        compiler_params=_cparams("arbitrary"),
        name="moba_sample",
    )(sel_pages, q, k_new, v_new, *([cache_k] * n_sel), *([cache_v] * n_sel))


def kernel(x_prompt, x_sample, state_conv, cache_k, cache_v, page_table, norm_mix, norm_ffn, norm_kv,
           w_conv_in, w_conv, w_conv_out, w_kv, k_norm, w_q, q_norm, w_o, w_router_group,
           b_router_group, w_router_expert, b_router_expert, w_expert_gate, w_expert_up, w_expert_down):
    bp, tp, d = x_prompt.shape
    bs = x_sample.shape[0]
    n_heads = d // HEAD_DIM
    page = cache_k.shape[1]
    kn = k_norm[None, :]
    qn = q_norm[0][None, :]

    def moe(x, layer, tm, blk):
        return _moe(x, layer, norm_ffn[layer][None, :], w_router_group[layer], b_router_group[layer],
                    w_router_expert[layer], b_router_expert[layer],
                    w_expert_gate, w_expert_up, w_expert_down, tm=tm, blk=blk)

    x1, conv_p = _mixer_prompt(x_prompt, norm_mix[0][None, :], w_conv_in[0], w_conv[0], w_conv_out[0])
    x2 = moe(x1.reshape(bp * tp, d), 0, 512, 256)
    k_p, v_p, kb, vt, qt, bias = _kvq_prompt(x2.reshape(bp, tp, d), norm_kv[None, :], norm_mix[1][None, :],
                                             w_kv, w_q[0], kn, qn)
    attn = _moba_prompt(qt, bias, kb, vt)
    x3 = _proj_res(x2, attn.reshape(bp * tp, d), w_o[0], 512, False)
    y_p = moe(x3, 1, 512, 256).reshape(bp, tp, d)

    xs = x_sample.reshape(bs, d)
    st = state_conv[0]
    x1s, up_s = _mixer_sample(xs, st[:, 0, :], st[:, 1, :], norm_mix[0][None, :],
                              w_conv_in[0], w_conv[0], w_conv_out[0])
    conv_s = jnp.stack([st[:, 1, :], up_s], axis=1)[None]
    x2s = moe(x1s, 0, bs, 16)
    k_s, v_s, q_s = _kvq_sample(x2s, norm_kv[None, :], norm_mix[1][None, :], w_kv, w_q[0], kn, qn)
    km = _page_means(cache_k, page_table)
    gate = jnp.einsum('shd,snhd->shn', q_s.reshape(bs, n_heads, HEAD_DIM), km, precision=HI)
    _, top = lax.top_k(gate, MOBA_TOPK)
    ppb = MOBA_BLOCK // page
    sel_logical = (top[..., None] * ppb + jnp.arange(ppb, dtype=top.dtype)).reshape(bs, n_heads, -1)
    sel_pages = jnp.take_along_axis(page_table[:, None, :], sel_logical.astype(I32), axis=2)
    rows = lambda z: z.reshape(bs * n_heads, 1, HEAD_DIM)
    attn_s = _moba_sample(rows(q_s), rows(k_s), rows(v_s), cache_k, cache_v,
                          sel_pages.reshape(-1).astype(I32))
    x3s = _proj_res(x2s, attn_s.reshape(bs, d), w_o[0], bs, True)
    y_s = moe(x3s, 1, bs, 16).reshape(bs, 1, d)

    return (y_p, y_s, conv_p[None],
            k_p.reshape(bp, tp, n_heads, HEAD_DIM), v_p.reshape(bp, tp, n_heads, HEAD_DIM),
            conv_s,
            k_s.reshape(bs, 1, n_heads, HEAD_DIM), v_s.reshape(bs, 1, n_heads, HEAD_DIM))
```

```python
import functools

import jax
import jax.numpy as jnp
from jax import lax
from jax.experimental import pallas as pl
from jax.experimental.pallas import tpu as pltpu

F32 = jnp.float32
BF16 = jnp.bfloat16
I32 = jnp.int32

EPS = 1e-6
HEAD_DIM = 128
MOBA_BLOCK = 256
MOBA_TOPK = 3
N_GROUPS = 4
EXPERTS_PER_GROUP = 8
EXPERT_TOPK = 2
CONV_W = 3

LANE = 128
SUBLANE = 8
VMEM_LIMIT = 48 * 1024 * 1024
NEG = -1e30
BIG_IDX = 1 << 20
LOG2_E = 1.4426950408889634
QK_SCALE = HEAD_DIM ** -0.5
QK_SCALE_LOG2 = QK_SCALE * LOG2_E


def _cparams(*sem):
    return pltpu.CompilerParams(dimension_semantics=sem, vmem_limit_bytes=VMEM_LIMIT)


def _rms(x, g):
    ms = jnp.mean(x * x, axis=-1, keepdims=True)
    return x * lax.rsqrt(ms + EPS) * g


def _dot(a, b):
    return jnp.dot(a.astype(BF16), b.astype(BF16), preferred_element_type=F32)


def _mixer_prompt_kernel(x_ref, g_ref, win_ref, wc_ref, wout_ref, o_ref, st_ref, buf_ref, *, tm, d):
    @pl.when(pl.program_id(1) == 0)
    def _():
        buf_ref[0:SUBLANE, :] = jnp.zeros((SUBLANE, d), F32)

    x = x_ref[...]
    h = _rms(x, g_ref[...])
    u = _dot(h, win_ref[...])
    bg = u[:, :d]
    up = u[:, d:2 * d] * u[:, 2 * d:]
    buf_ref[SUBLANE:SUBLANE + tm, :] = up
    up1 = buf_ref[SUBLANE - 1:SUBLANE - 1 + tm, :]
    up2 = buf_ref[SUBLANE - 2:SUBLANE - 2 + tm, :]
    wc = wc_ref[...]
    y = wc[0:1] * up2 + wc[1:2] * up1 + wc[2:3] * up
    o_ref[...] = x + _dot(bg * y, wout_ref[...])
    last2 = buf_ref[tm + SUBLANE - 2:tm + SUBLANE, :]
    st_ref[...] = last2
    buf_ref[SUBLANE - 2:SUBLANE, :] = last2


def _mixer_prompt(x, g, w_in, w_c, w_out, tm=256):
    b, t, d = x.shape
    kern = functools.partial(_mixer_prompt_kernel, tm=tm, d=d)
    return pl.pallas_call(
        kern,
        grid=(b, t // tm),
        in_specs=[
            pl.BlockSpec((None, tm, d), lambda i, j: (i, j, 0)),
            pl.BlockSpec((1, d), lambda i, j: (0, 0)),
            pl.BlockSpec((d, 3 * d), lambda i, j: (0, 0)),
            pl.BlockSpec((CONV_W, d), lambda i, j: (0, 0)),
            pl.BlockSpec((d, d), lambda i, j: (0, 0)),
        ],
        out_specs=[
            pl.BlockSpec((None, tm, d), lambda i, j: (i, j, 0)),
            pl.BlockSpec((None, CONV_W - 1, d), lambda i, j: (i, 0, 0)),
        ],
        out_shape=[
            jax.ShapeDtypeStruct((b, t, d), F32),
            jax.ShapeDtypeStruct((b, CONV_W - 1, d), F32),
        ],
        scratch_shapes=[pltpu.VMEM((tm + SUBLANE, d), F32)],
        compiler_params=_cparams("arbitrary", "arbitrary"),
        name="mixer_prompt",
    )(x, g, w_in.astype(BF16), w_c, w_out.astype(BF16))


def _mixer_sample_kernel(x_ref, p0_ref, p1_ref, g_ref, win_ref, wc_ref, wout_ref, o_ref, up_ref, *, d):
    x = x_ref[...]
    h = _rms(x, g_ref[...])
    u = _dot(h, win_ref[...])
    bg = u[:, :d]
    up = u[:, d:2 * d] * u[:, 2 * d:]
    wc = wc_ref[...]
    y = wc[0:1] * p0_ref[...] + wc[1:2] * p1_ref[...] + wc[2:3] * up
    o_ref[...] = x + _dot(bg * y, wout_ref[...])
    up_ref[...] = up


def _mixer_sample(x, p0, p1, g, w_in, w_c, w_out):
    n, d = x.shape
    return pl.pallas_call(
        functools.partial(_mixer_sample_kernel, d=d),
        out_shape=[jax.ShapeDtypeStruct((n, d), F32), jax.ShapeDtypeStruct((n, d), F32)],
        compiler_params=pltpu.CompilerParams(vmem_limit_bytes=VMEM_LIMIT),
        name="mixer_sample",
    )(x, p0, p1, g, w_in, w_c, w_out)


def _router_kernel(x_ref, g_ref, wr_ref, br_ref, xn_ref, info_ref):
    xn = _rms(x_ref[...], g_ref[...])
    xn_ref[...] = xn.astype(BF16)
    lg = _dot(xn, wr_ref[...]) + br_ref[...]
    lane = lax.broadcasted_iota(I32, lg.shape, 1)
    ninf = jnp.float32(-jnp.inf)
    is_g = lane < N_GROUPS
    gl = jnp.where(is_g, lg, ninf)
    gmax = jnp.max(gl, axis=-1, keepdims=True)
    gsel = jnp.min(jnp.where(gl == gmax, lane, BIG_IDX), axis=-1, keepdims=True)
    gsum = jnp.sum(jnp.where(is_g, jnp.exp(gl - gmax), 0.0), axis=-1, keepdims=True)
    p_group = 1.0 / gsum
    lo = N_GROUPS + gsel * EXPERTS_PER_GROUP
    in_grp = (lane >= lo) & (lane < lo + EXPERTS_PER_GROUP)
    el = jnp.where(in_grp, lg, ninf)
    e1 = jnp.max(el, axis=-1, keepdims=True)
    i1 = jnp.min(jnp.where(el == e1, lane, BIG_IDX), axis=-1, keepdims=True)
    el2 = jnp.where(lane == i1, ninf, el)
    e2 = jnp.max(el2, axis=-1, keepdims=True)
    i2 = jnp.min(jnp.where(el2 == e2, lane, BIG_IDX), axis=-1, keepdims=True)
    r = jnp.exp(e2 - e1)
    w1 = p_group / (1.0 + r)
    w2 = w1 * r
    info = jnp.where(lane == 0, (i1 - N_GROUPS).astype(F32),
                     jnp.where(lane == 1, (i2 - N_GROUPS).astype(F32),
                               jnp.where(lane == 2, w1, jnp.where(lane == 3, w2, 0.0))))
    info_ref[...] = info


def _router(x, g, wr, br, tm):
    n, d = x.shape
    return pl.pallas_call(
        _router_kernel,
        grid=(n // tm,),
        in_specs=[
            pl.BlockSpec((tm, d), lambda i: (i, 0)),
            pl.BlockSpec((1, d), lambda i: (0, 0)),
            pl.BlockSpec((d, LANE), lambda i: (0, 0)),
            pl.BlockSpec((1, LANE), lambda i: (0, 0)),
        ],
        out_specs=[
            pl.BlockSpec((tm, d), lambda i: (i, 0)),
            pl.BlockSpec((tm, LANE), lambda i: (i, 0)),
        ],
        out_shape=[jax.ShapeDtypeStruct((n, d), BF16), jax.ShapeDtypeStruct((n, LANE), F32)],
        compiler_params=_cparams("arbitrary"),
        name="router",
    )(x, g, wr, br)


def _experts_kernel(be_ref, nu_ref, xs_ref, wg_ref, wu_ref, wd_ref, o_ref, wgb, wub, wdb):
    i = pl.program_id(0)
    e = be_ref[i]
    prev = be_ref[jnp.maximum(i - 1, 0)]

    @pl.when((i == 0) | (e != prev))
    def _():
        wgb[...] = wg_ref[...].astype(BF16)
        wub[...] = wu_ref[...].astype(BF16)
        wdb[...] = wd_ref[...].astype(BF16)

    @pl.when(i < nu_ref[0])
    def _():
        xb = xs_ref[...]
        h1 = jnp.dot(xb, wgb[...], preferred_element_type=F32)
        h2 = jnp.dot(xb, wub[...], preferred_element_type=F32)
        h = h1 * jax.nn.sigmoid(h1) * h2
        o_ref[...] = jnp.dot(h.astype(BF16), wdb[...], preferred_element_type=F32)

    @pl.when(i >= nu_ref[0])
    def _():
        o_ref[...] = jnp.zeros(o_ref.shape, F32)


def _experts(xs, blk_expert, n_used, wg, wu, wd, layer, blk):
    p, d = xs.shape
    de = wg.shape[-1]
    n_blk = p // blk
    grid_spec = pltpu.PrefetchScalarGridSpec(
        num_scalar_prefetch=2,
        grid=(n_blk,),
        in_specs=[
            pl.BlockSpec((blk, d), lambda i, be, nu: (jnp.minimum(i, nu[0] - 1), 0)),
            pl.BlockSpec((None, None, d, de), lambda i, be, nu: (layer, be[i], 0, 0)),
            pl.BlockSpec((None, None, d, de), lambda i, be, nu: (layer, be[i], 0, 0)),
            pl.BlockSpec((None, None, de, d), lambda i, be, nu: (layer, be[i], 0, 0)),
        ],
        out_specs=pl.BlockSpec((blk, d), lambda i, be, nu: (i, 0)),
        scratch_shapes=[pltpu.VMEM((d, de), BF16), pltpu.VMEM((d, de), BF16), pltpu.VMEM((de, d), BF16)],
    )
    return pl.pallas_call(
        _experts_kernel,
        grid_spec=grid_spec,
        out_shape=jax.ShapeDtypeStruct((p, d), F32),
        compiler_params=_cparams("arbitrary"),
        name="experts",
    )(blk_expert, n_used, xs, wg, wu, wd)


def _moe(x, layer, g, w_rg, b_rg, w_re, b_re, wg, wu, wd, *, tm, blk):
    n, d = x.shape
    n_exp = wg.shape[1]
    pad = LANE - N_GROUPS - n_exp
    wr = jnp.concatenate([w_rg, w_re, jnp.zeros((d, pad), F32)], axis=1)
    br = jnp.concatenate([b_rg, b_re, jnp.zeros((pad,), F32)])[None, :]
    xn, info = _router(x, g, wr, br, tm)
    eid = info[:, 0:EXPERT_TOPK].astype(I32)
    wts = info[:, EXPERT_TOPK:2 * EXPERT_TOPK]

    a = n * EXPERT_TOPK
    e_flat = eid.reshape(a)
    onehot = (e_flat[:, None] == jnp.arange(n_exp, dtype=I32)[None, :]).astype(I32)
    csum = jnp.cumsum(onehot, axis=0)
    counts = csum[-1]
    rank = jnp.sum(onehot * csum, axis=1) - 1
    padded = ((counts + blk - 1) // blk) * blk
    pad_end = jnp.cumsum(padded)
    pad_start = pad_end - padded
    dest = pad_start[e_flat] + rank
    n_blk = -(-a // blk) + n_exp
    n_used = (pad_end[-1] // blk).astype(I32)
    blk_ids = jnp.minimum(jnp.arange(n_blk, dtype=I32), n_used - 1)
    blk_expert = jnp.clip(jnp.searchsorted(pad_end, blk_ids * blk, side='right'), 0, n_exp - 1).astype(I32)
    tok = jnp.arange(a, dtype=I32) // EXPERT_TOPK
    row_tok = jnp.zeros((n_blk * blk,), I32).at[dest].set(tok)
    xs = jnp.take(xn, row_tok, axis=0)
    ys = _experts(xs, blk_expert, n_used.reshape(1), wg, wu, wd, layer, blk)
    dest2 = dest.reshape(n, EXPERT_TOPK)
    out = x
    for k in range(EXPERT_TOPK):
        out = out + wts[:, k:k + 1] * jnp.take(ys, dest2[:, k], axis=0)
    return out


def _head_rms(x, g):
    return x * lax.rsqrt(jnp.mean(x * x, axis=-1, keepdims=True) + EPS) * g


def _kvq_prompt_kernel(x_ref, gkv_ref, gq_ref, wkv_ref, wq_ref, kn_ref, qn_ref,
                       k_ref, v_ref, kb_ref, vt_ref, qt_ref, bias_ref, km_ref, *, n_heads, nb):
    i = pl.program_id(1)
    d = n_heads * HEAD_DIM

    @pl.when(i == 0)
    def _():
        km_ref[...] = jnp.zeros(km_ref.shape, F32)

    x = x_ref[...]
    kv = _dot(_rms(x, gkv_ref[...]), wkv_ref[...])
    q = _dot(_rms(x, gq_ref[...]), wq_ref[...])
    v_ref[...] = kv[:, d:]
    tq = x.shape[0]
    blk_iota = lax.broadcasted_iota(I32, (nb, tq), 0)
    row_iota = lax.broadcasted_iota(I32, (nb, HEAD_DIM), 0)
    ninf = jnp.float32(-jnp.inf)
    for h in range(n_heads):
        hs = slice(h * HEAD_DIM, (h + 1) * HEAD_DIM)
        kh = _head_rms(kv[:, hs], kn_ref[...])
        k_ref[:, hs] = kh
        kb_ref[h] = kh.astype(BF16)
        vt_ref[h] = kv[:, d + h * HEAD_DIM:d + (h + 1) * HEAD_DIM].T.astype(BF16)
        qt = _head_rms(q[:, hs], qn_ref[...]).T
        qt_ref[h] = (qt * QK_SCALE_LOG2).astype(BF16)
        kmh = km_ref[:, hs]
        gate = _dot(kmh, qt)
        gate = jnp.where(blk_iota < i, gate, ninf)
        sel = jnp.zeros((nb, tq), jnp.bool_)
        for _ in range(MOBA_TOPK):
            m = jnp.max(gate, axis=0, keepdims=True)
            idx = jnp.min(jnp.where(gate == m, blk_iota, BIG_IDX), axis=0, keepdims=True)
            hit = blk_iota == idx
            sel = sel | (hit & (m > ninf))
            gate = jnp.where(hit, ninf, gate)
        bias_ref[h] = jnp.where(sel, 0.0, NEG)
        kmean = jnp.mean(kh, axis=0, keepdims=True)
        km_ref[:, hs] = jnp.where(row_iota == i, kmean, kmh)


def _kvq_prompt(x, gkv, gq, w_kv, w_q, kn, qn):
    b, t, d = x.shape
    n_heads = d // HEAD_DIM
    tq = MOBA_BLOCK
    nb = t // tq
    kern = functools.partial(_kvq_prompt_kernel, n_heads=n_heads, nb=nb)
    const = lambda i, j: (0, 0)
    return pl.pallas_call(
        kern,
        grid=(b, nb),
        in_specs=[
            pl.BlockSpec((None, tq, d), lambda i, j: (i, j, 0)),
            pl.BlockSpec((1, d), const),
            pl.BlockSpec((1, d), const),
            pl.BlockSpec((d, 2 * d), const),
            pl.BlockSpec((d, d), const),
            pl.BlockSpec((1, HEAD_DIM), const),
            pl.BlockSpec((1, HEAD_DIM), const),
        ],
        out_specs=[
            pl.BlockSpec((None, tq, d), lambda i, j: (i, j, 0)),
            pl.BlockSpec((None, tq, d), lambda i, j: (i, j, 0)),
            pl.BlockSpec((None, n_heads, tq, HEAD_DIM), lambda i, j: (i, 0, j, 0)),
            pl.BlockSpec((None, n_heads, None, HEAD_DIM, tq), lambda i, j: (i, 0, j, 0, 0)),
            pl.BlockSpec((None, n_heads, None, HEAD_DIM, tq), lambda i, j: (i, 0, j, 0, 0)),
            pl.BlockSpec((None, n_heads, None, nb, tq), lambda i, j: (i, 0, j, 0, 0)),
        ],
        out_shape=[
            jax.ShapeDtypeStruct((b, t, d), F32),
            jax.ShapeDtypeStruct((b, t, d), F32),
            jax.ShapeDtypeStruct((b, n_heads, t, HEAD_DIM), BF16),
            jax.ShapeDtypeStruct((b, n_heads, nb, HEAD_DIM, tq), BF16),
            jax.ShapeDtypeStruct((b, n_heads, nb, HEAD_DIM, tq), BF16),
            jax.ShapeDtypeStruct((b, n_heads, nb, nb, tq), F32),
        ],
        scratch_shapes=[pltpu.VMEM((nb, d), F32)],
        compiler_params=_cparams("arbitrary", "arbitrary"),
        name="kvq_prompt",
    )(x, gkv, gq, w_kv.astype(BF16), w_q.astype(BF16), kn, qn)


def _moba_prompt_kernel(qt_ref, bias_ref, kb_ref, vt_ref, o_ref, *, tq, grp, sub):
    i = pl.program_id(2)
    qt = qt_ref[...]

    def scores(n):
        off = pl.multiple_of(n * tq, tq)
        return jnp.dot(kb_ref[pl.ds(off, tq), :], qt, preferred_element_type=F32)

    def update(blocks, m, l, acc):
        m_new = m
        for s, _ in blocks:
            m_new = jnp.maximum(m_new, jnp.max(s, axis=0, keepdims=True))
        alpha = jnp.exp2(m - m_new)
        l = alpha * l
        acc = alpha * acc
        for s, n in blocks:
            p = jnp.exp2(s - m_new)
            l = l + jnp.sum(p, axis=0, keepdims=True)
            acc = acc + jnp.dot(vt_ref[n], p.astype(BF16), preferred_element_type=F32)
        return m_new, l, acc

    s = scores(i)
    key_i = lax.broadcasted_iota(I32, s.shape, 0)
    qry_i = lax.broadcasted_iota(I32, s.shape, 1)
    blocks = [(jnp.where(key_i <= qry_i, s, NEG), i)]
    for c in range(grp - 1):
        blocks.append((scores(c) + bias_ref[c:c + 1, :], c))
    def update_all(blocks, carry):
        for c in range(0, grp, sub):
            carry = update(blocks[c:c + sub], *carry)
        return carry

    carry = update_all(blocks, (jnp.full((1, tq), NEG, F32), jnp.zeros((1, tq), F32),
                                jnp.zeros((HEAD_DIM, tq), F32)))

    def body(g, carry):
        base = (grp - 1) + g * grp
        blocks = []
        for c in range(grp):
            n = base + c
            blocks.append((scores(n) + bias_ref[pl.ds(n, 1), :], n))
        return update_all(blocks, carry)

    n_groups = (jnp.maximum(i - (grp - 1), 0) + grp - 1) // grp
    _, l, acc = lax.fori_loop(0, n_groups, body, carry)
    o_ref[...] = (acc / l).T.astype(o_ref.dtype)


def _moba_prompt(qt, bias, kb, vt, grp=4, sub=1):
    b, n_heads, nb, _, tq = qt.shape
    t = nb * tq
    assert nb % grp == 0
    return pl.pallas_call(
        functools.partial(_moba_prompt_kernel, tq=tq, grp=grp, sub=sub),
        grid=(b, n_heads, nb),
        in_specs=[
            pl.BlockSpec((None, None, None, HEAD_DIM, tq), lambda bi, h, i: (bi, h, i, 0, 0)),
            pl.BlockSpec((None, None, None, nb, tq), lambda bi, h, i: (bi, h, i, 0, 0)),
            pl.BlockSpec((None, None, t, HEAD_DIM), lambda bi, h, i: (bi, h, 0, 0)),
            pl.BlockSpec((None, None, nb, HEAD_DIM, tq), lambda bi, h, i: (bi, h, 0, 0, 0)),
        ],
        out_specs=pl.BlockSpec((None, tq, HEAD_DIM), lambda bi, h, i: (bi, i, h)),
        out_shape=jax.ShapeDtypeStruct((b, t, n_heads * HEAD_DIM), BF16),
        compiler_params=_cparams("arbitrary", "arbitrary", "arbitrary"),
        name="moba_prompt",
    )(qt, bias, kb, vt)


def _proj_res_kernel(x_ref, a_ref, w_ref, o_ref):
    o_ref[...] = x_ref[...] + _dot(a_ref[...], w_ref[...])


def _proj_res(x, a, w, tm):
    n, d = x.shape
    w = w.astype(BF16)
    return pl.pallas_call(
        _proj_res_kernel,
        grid=(n // tm,),
        in_specs=[
            pl.BlockSpec((tm, d), lambda i: (i, 0)),
            pl.BlockSpec((tm, a.shape[1]), lambda i: (i, 0)),
            pl.BlockSpec(w.shape, lambda i: (0, 0)),
        ],
        out_specs=pl.BlockSpec((tm, d), lambda i: (i, 0)),
        out_shape=jax.ShapeDtypeStruct((n, d), F32),
        compiler_params=_cparams("arbitrary"),
        name="proj_res",
    )(x, a, w)


def _kvq_sample_kernel(x_ref, gkv_ref, gq_ref, wkv_ref, wq_ref, kn_ref, qn_ref, k_ref, v_ref, q_ref,
                       *, n_heads):
    d = n_heads * HEAD_DIM
    x = x_ref[...]
    kv = _dot(_rms(x, gkv_ref[...]), wkv_ref[...])
    q = _dot(_rms(x, gq_ref[...]), wq_ref[...])
    v_ref[...] = kv[:, d:]
    for h in range(n_heads):
        hs = slice(h * HEAD_DIM, (h + 1) * HEAD_DIM)
        k_ref[:, hs] = _head_rms(kv[:, hs], kn_ref[...])
        q_ref[:, hs] = _head_rms(q[:, hs], qn_ref[...])


def _kvq_sample(x, gkv, gq, w_kv, w_q, kn, qn):
    n, d = x.shape
    sds = jax.ShapeDtypeStruct((n, d), F32)
    return pl.pallas_call(
        functools.partial(_kvq_sample_kernel, n_heads=d // HEAD_DIM),
        out_shape=[sds, sds, sds],
        compiler_params=pltpu.CompilerParams(vmem_limit_bytes=VMEM_LIMIT),
        name="kvq_sample",
    )(x, gkv, gq, w_kv, w_q, kn, qn)


def _sample_blocks_kernel(pt_ref, q_ref, *refs, pages_per_step, pages_per_block):
    del pt_ref
    k_refs = refs[:pages_per_step]
    v_refs = refs[pages_per_step:2 * pages_per_step]
    km_ref, m_ref, l_ref, o_ref = refs[2 * pages_per_step:]
    q = q_ref[...] * QK_SCALE
    n_keys = pages_per_block * k_refs[0].shape[0]
    for c in range(pages_per_step // pages_per_block):
        pages = range(c * pages_per_block, (c + 1) * pages_per_block)
        ks = [k_refs[r][...] for r in pages]
        s = [jnp.sum(kp * q[None], axis=-1, keepdims=True) for kp in ks]
        m = s[0].max(axis=0)
        ksum = ks[0].sum(axis=0)
        for sr, kp in zip(s[1:], ks[1:]):
            m = jnp.maximum(m, sr.max(axis=0))
            ksum = ksum + kp.sum(axis=0)
        l = jnp.zeros_like(m)
        o = jnp.zeros(q.shape, F32)
        for sr, r in zip(s, pages):
            p = jnp.exp(sr - m[None])
            l = l + p.sum(axis=0)
            o = o + (p * v_refs[r][...]).sum(axis=0)
        km_ref[c] = ksum * (1.0 / n_keys)
        m_ref[c] = jnp.broadcast_to(m, q.shape)
        l_ref[c] = jnp.broadcast_to(l, q.shape)
        o_ref[c] = o


def _sample_blocks(q, cache_k, cache_v, page_table, pages_per_step=8):
    n_seq, n_pages = page_table.shape
    _, page, n_heads, hd = cache_k.shape
    ppb = MOBA_BLOCK // page
    assert MOBA_BLOCK % page == 0 and n_pages % pages_per_step == 0 and pages_per_step % ppb == 0
    steps = n_pages // pages_per_step
    bps = pages_per_step // ppb

    def page_spec(c):
        return pl.BlockSpec((None, page, n_heads, hd),
                            lambda s, g, pt: (pt[(s * steps + g) * pages_per_step + c], 0, 0, 0))

    blk_out = pl.BlockSpec((None, bps, n_heads, hd), lambda s, g, pt: (s, g, 0, 0))
    grid_spec = pltpu.PrefetchScalarGridSpec(
        num_scalar_prefetch=1,
        grid=(n_seq, steps),
        in_specs=[pl.BlockSpec((None, n_heads, hd), lambda s, g, pt: (s, 0, 0))]
        + [page_spec(c) for c in range(pages_per_step)] * 2,
        out_specs=[blk_out] * 4,
    )
    sds = jax.ShapeDtypeStruct((n_seq, n_pages // ppb, n_heads, hd), F32)
    return pl.pallas_call(
        functools.partial(_sample_blocks_kernel, pages_per_step=pages_per_step, pages_per_block=ppb),
        grid_spec=grid_spec,
        out_shape=[sds] * 4,
        compiler_params=_cparams("arbitrary", "arbitrary"),
        name="sample_blocks",
    )(page_table.reshape(-1), q, *([cache_k] * pages_per_step), *([cache_v] * pages_per_step))


def _sample_combine_kernel(q_ref, kn_ref, vn_ref, km_ref, m_ref, l_ref, o_ref, out_ref):
    q = q_ref[...]
    qb = q.astype(BF16).astype(F32)
    gate = jnp.sum(km_ref[...].astype(BF16).astype(F32) * qb[None], axis=-1, keepdims=True)
    blk_iota = lax.broadcasted_iota(I32, gate.shape, 0)
    ninf = jnp.float32(-jnp.inf)
    sel = jnp.zeros(gate.shape, jnp.bool_)
    for _ in range(MOBA_TOPK):
        g_max = jnp.max(gate, axis=0, keepdims=True)
        idx = jnp.min(jnp.where(gate == g_max, blk_iota, BIG_IDX), axis=0, keepdims=True)
        hit = blk_iota == idx
        sel = sel | (hit & (g_max > ninf))
        gate = jnp.where(hit, ninf, gate)
    s_new = jnp.sum(q * QK_SCALE * kn_ref[...], axis=-1, keepdims=True)
    mb = m_ref[...]
    m_tot = jnp.maximum(jnp.max(jnp.where(sel, mb, ninf), axis=0), s_new)
    w = jnp.where(sel, jnp.exp(mb - m_tot[None]), 0.0)
    p_new = jnp.exp(s_new - m_tot)
    l = jnp.sum(w * l_ref[...], axis=0) + p_new
    o = jnp.sum(w * o_ref[...], axis=0) + p_new * vn_ref[...]
    out_ref[...] = o / l


def _sample_combine(q, k_new, v_new, km, mb, lb, ob):
    n_seq, nblk, n_heads, hd = km.shape
    row = pl.BlockSpec((None, n_heads, hd), lambda s: (s, 0, 0))
    blk = pl.BlockSpec((None, nblk, n_heads, hd), lambda s: (s, 0, 0, 0))
    return pl.pallas_call(
        _sample_combine_kernel,
        grid=(n_seq,),
        in_specs=[row, row, row, blk, blk, blk, blk],
        out_specs=row,
        out_shape=jax.ShapeDtypeStruct((n_seq, n_heads, hd), F32),
        compiler_params=_cparams("arbitrary"),
        name="sample_combine",
    )(q, k_new, v_new, km, mb, lb, ob)


def kernel(x_prompt, x_sample, state_conv, cache_k, cache_v, page_table, norm_mix, norm_ffn, norm_kv,
           w_conv_in, w_conv, w_conv_out, w_kv, k_norm, w_q, q_norm, w_o, w_router_group,
           b_router_group, w_router_expert, b_router_expert, w_expert_gate, w_expert_up, w_expert_down):
    bp, tp, d = x_prompt.shape
    bs = x_sample.shape[0]
    n_heads = d // HEAD_DIM
    kn = k_norm[None, :]
    qn = q_norm[0][None, :]

    def moe(x, layer, tm, blk):
        return _moe(x, layer, norm_ffn[layer][None, :], w_router_group[layer], b_router_group[layer],
                    w_router_expert[layer], b_router_expert[layer],
                    w_expert_gate, w_expert_up, w_expert_down, tm=tm, blk=blk)

    x1, conv_p = _mixer_prompt(x_prompt, norm_mix[0][None, :], w_conv_in[0], w_conv[0], w_conv_out[0])
    x2 = moe(x1.reshape(bp * tp, d), 0, 512, 256)
    k_p, v_p, kb, vt, qt, bias = _kvq_prompt(x2.reshape(bp, tp, d), norm_kv[None, :], norm_mix[1][None, :],
                                             w_kv, w_q[0], kn, qn)
    attn = _moba_prompt(qt, bias, kb, vt)
    x3 = _proj_res(x2, attn.reshape(bp * tp, d), w_o[0], 512)
    y_p = moe(x3, 1, 512, 256).reshape(bp, tp, d)

    xs = x_sample.reshape(bs, d)
    st = state_conv[0]
    x1s, up_s = _mixer_sample(xs, st[:, 0, :], st[:, 1, :], norm_mix[0][None, :],
                              w_conv_in[0], w_conv[0], w_conv_out[0])
    conv_s = jnp.stack([st[:, 1, :], up_s], axis=1)[None]
    x2s = moe(x1s, 0, bs, 16)
    k_s, v_s, q_s = _kvq_sample(x2s, norm_kv[None, :], norm_mix[1][None, :], w_kv, w_q[0], kn, qn)
    heads = lambda z: z.reshape(bs, n_heads, HEAD_DIM)
    km, mb, lb, ob = _sample_blocks(heads(q_s), cache_k, cache_v, page_table)
    attn_s = _sample_combine(heads(q_s), heads(k_s), heads(v_s), km, mb, lb, ob)
    x3s = _proj_res(x2s, attn_s.reshape(bs, d), w_o[0], bs)
    y_s = moe(x3s, 1, bs, 16).reshape(bs, 1, d)

    return (y_p, y_s, conv_p[None],
            k_p.reshape(bp, tp, n_heads, HEAD_DIM), v_p.reshape(bp, tp, n_heads, HEAD_DIM),
            conv_s,
            k_s.reshape(bs, 1, n_heads, HEAD_DIM), v_s.reshape(bs, 1, n_heads, HEAD_DIM))
```

```python
import functools

import jax
import jax.numpy as jnp
from jax import lax
from jax.experimental import pallas as pl
from jax.experimental.pallas import tpu as pltpu

F32 = jnp.float32
BF16 = jnp.bfloat16
I32 = jnp.int32

EPS = 1e-6
HEAD_DIM = 128
MOBA_BLOCK = 256
MOBA_TOPK = 3
N_GROUPS = 4
EXPERTS_PER_GROUP = 8
EXPERT_TOPK = 2
CONV_W = 3

LANE = 128
SUBLANE = 8
VMEM_LIMIT = 48 * 1024 * 1024
NEG = -1e30
BIG_IDX = 1 << 20
LOG2_E = 1.4426950408889634
QK_SCALE = HEAD_DIM ** -0.5
QK_SCALE_LOG2 = QK_SCALE * LOG2_E


def _cparams(*sem):
    return pltpu.CompilerParams(dimension_semantics=sem, vmem_limit_bytes=VMEM_LIMIT)


def _rms(x, g):
    ms = jnp.mean(x * x, axis=-1, keepdims=True)
    return x * lax.rsqrt(ms + EPS) * g


def _dot(a, b):
    return jnp.dot(a.astype(BF16), b.astype(BF16), preferred_element_type=F32)


def _mixer_prompt_kernel(x_ref, g_ref, win_ref, wc_ref, wout_ref, o_ref, st_ref, buf_ref, *, tm, d):
    @pl.when(pl.program_id(1) == 0)
    def _():
        buf_ref[0:SUBLANE, :] = jnp.zeros((SUBLANE, d), F32)

    x = x_ref[...]
    h = _rms(x, g_ref[...])
    u = _dot(h, win_ref[...])
    bg = u[:, :d]
    up = u[:, d:2 * d] * u[:, 2 * d:]
    buf_ref[SUBLANE:SUBLANE + tm, :] = up
    up1 = buf_ref[SUBLANE - 1:SUBLANE - 1 + tm, :]
    up2 = buf_ref[SUBLANE - 2:SUBLANE - 2 + tm, :]
    wc = wc_ref[...]
    y = wc[0:1] * up2 + wc[1:2] * up1 + wc[2:3] * up
    o_ref[...] = x + _dot(bg * y, wout_ref[...])
    last2 = buf_ref[tm + SUBLANE - 2:tm + SUBLANE, :]
    st_ref[...] = last2
    buf_ref[SUBLANE - 2:SUBLANE, :] = last2


def _mixer_prompt(x, g, w_in, w_c, w_out, tm=256):
    b, t, d = x.shape
    kern = functools.partial(_mixer_prompt_kernel, tm=tm, d=d)
    return pl.pallas_call(
        kern,
        grid=(b, t // tm),
        in_specs=[
            pl.BlockSpec((None, tm, d), lambda i, j: (i, j, 0)),
            pl.BlockSpec((1, d), lambda i, j: (0, 0)),
            pl.BlockSpec((d, 3 * d), lambda i, j: (0, 0)),
            pl.BlockSpec((CONV_W, d), lambda i, j: (0, 0)),
            pl.BlockSpec((d, d), lambda i, j: (0, 0)),
        ],
        out_specs=[
            pl.BlockSpec((None, tm, d), lambda i, j: (i, j, 0)),
            pl.BlockSpec((None, CONV_W - 1, d), lambda i, j: (i, 0, 0)),
        ],
        out_shape=[
            jax.ShapeDtypeStruct((b, t, d), F32),
            jax.ShapeDtypeStruct((b, CONV_W - 1, d), F32),
        ],
        scratch_shapes=[pltpu.VMEM((tm + SUBLANE, d), F32)],
        compiler_params=_cparams("arbitrary", "arbitrary"),
        name="mixer_prompt",
    )(x, g, w_in.astype(BF16), w_c, w_out.astype(BF16))


def _mixer_sample_kernel(x_ref, p0_ref, p1_ref, g_ref, win_ref, wc_ref, wout_ref, o_ref, up_ref, *, d):
    x = x_ref[...]
    h = _rms(x, g_ref[...])
    u = _dot(h, win_ref[...])
    bg = u[:, :d]
    up = u[:, d:2 * d] * u[:, 2 * d:]
    wc = wc_ref[...]
    y = wc[0:1] * p0_ref[...] + wc[1:2] * p1_ref[...] + wc[2:3] * up
    o_ref[...] = x + _dot(bg * y, wout_ref[...])
    up_ref[...] = up


def _mixer_sample(x, p0, p1, g, w_in, w_c, w_out):
    n, d = x.shape
    return pl.pallas_call(
        functools.partial(_mixer_sample_kernel, d=d),
        out_shape=[jax.ShapeDtypeStruct((n, d), F32), jax.ShapeDtypeStruct((n, d), F32)],
        compiler_params=pltpu.CompilerParams(vmem_limit_bytes=VMEM_LIMIT),
        name="mixer_sample",
    )(x, p0, p1, g, w_in, w_c, w_out)


def _router_kernel(x_ref, g_ref, wr_ref, br_ref, info_ref, cnt_ref, run_ref):
    @pl.when(pl.program_id(0) == 0)
    def _():
        run_ref[...] = jnp.zeros(run_ref.shape, F32)

    xn = _rms(x_ref[...], g_ref[...])
    lg = _dot(xn, wr_ref[...]) + br_ref[...]
    lane = lax.broadcasted_iota(I32, lg.shape, 1)
    ninf = jnp.float32(-jnp.inf)
    is_g = lane < N_GROUPS
    gl = jnp.where(is_g, lg, ninf)
    gmax = jnp.max(gl, axis=-1, keepdims=True)
    gsel = jnp.min(jnp.where(gl == gmax, lane, BIG_IDX), axis=-1, keepdims=True)
    gsum = jnp.sum(jnp.where(is_g, jnp.exp(gl - gmax), 0.0), axis=-1, keepdims=True)
    p_group = 1.0 / gsum
    lo = N_GROUPS + gsel * EXPERTS_PER_GROUP
    in_grp = (lane >= lo) & (lane < lo + EXPERTS_PER_GROUP)
    el = jnp.where(in_grp, lg, ninf)
    e1 = jnp.max(el, axis=-1, keepdims=True)
    i1 = jnp.min(jnp.where(el == e1, lane, BIG_IDX), axis=-1, keepdims=True)
    el2 = jnp.where(lane == i1, ninf, el)
    e2 = jnp.max(el2, axis=-1, keepdims=True)
    i2 = jnp.min(jnp.where(el2 == e2, lane, BIG_IDX), axis=-1, keepdims=True)
    r = jnp.exp(e2 - e1)
    w1 = p_group / (1.0 + r)
    w2 = w1 * r
    eid1 = i1 - N_GROUPS
    eid2 = i2 - N_GROUPS
    hit1 = lane == eid1
    hit2 = lane == eid2
    onehot = jnp.where(hit1 | hit2, 1.0, 0.0)
    tm = onehot.shape[0]
    tri = jnp.where(lax.broadcasted_iota(I32, (tm, tm), 0) > lax.broadcasted_iota(I32, (tm, tm), 1), 1.0, 0.0)
    before = _dot(tri, onehot) + run_ref[0:1, :]
    rank1 = jnp.sum(jnp.where(hit1, before, 0.0), axis=-1, keepdims=True)
    rank2 = jnp.sum(jnp.where(hit2, before, 0.0), axis=-1, keepdims=True)
    total = run_ref[0:1, :] + jnp.sum(onehot, axis=0, keepdims=True)
    run_ref[0:1, :] = total
    cnt_ref[...] = total
    cols = (eid1.astype(F32), eid2.astype(F32), w1, w2, rank1, rank2)
    info = jnp.zeros(lg.shape, F32)
    for c, val in enumerate(cols):
        info = jnp.where(lane == c, val, info)
    info_ref[...] = info


ROUTE_EID, ROUTE_W, ROUTE_RANK = 0, 2, 4


def _router(x, g, wr, br, tm):
    n, d = x.shape
    return pl.pallas_call(
        _router_kernel,
        grid=(n // tm,),
        in_specs=[
            pl.BlockSpec((tm, d), lambda i: (i, 0)),
            pl.BlockSpec((1, d), lambda i: (0, 0)),
            pl.BlockSpec((d, LANE), lambda i: (0, 0)),
            pl.BlockSpec((1, LANE), lambda i: (0, 0)),
        ],
        out_specs=[
            pl.BlockSpec((tm, LANE), lambda i: (i, 0)),
            pl.BlockSpec((1, LANE), lambda i: (0, 0)),
        ],
        out_shape=[jax.ShapeDtypeStruct((n, LANE), F32), jax.ShapeDtypeStruct((1, LANE), F32)],
        scratch_shapes=[pltpu.VMEM((SUBLANE, LANE), F32)],
        compiler_params=_cparams("arbitrary"),
        name="router",
    )(x, g, wr, br)


def _row_copy(src, src_row, dst, dst_row, sem):
    return pltpu.make_async_copy(src.at[pl.ds(src_row, 1), :], dst.at[pl.ds(dst_row, 1), :], sem)


def _dispatch_kernel(dest_ref, x_ref, g_ref, zeros_ref, xs_ref, buf, sem, *, tm, n_steps):
    del zeros_ref
    i = pl.program_id(0)
    slot = i % 2

    def wait_rows(s):
        for _ in range(EXPERT_TOPK):
            pltpu.make_async_copy(buf.at[s], xs_ref.at[pl.ds(0, tm), :], sem.at[s]).wait()

    @pl.when(i >= 2)
    def _():
        wait_rows(slot)

    buf[slot] = _rms(x_ref[...], g_ref[...])

    def issue(r, carry):
        a = (i * tm + r) * EXPERT_TOPK
        for k in range(EXPERT_TOPK):
            _row_copy(buf.at[slot], r, xs_ref, dest_ref[a + k], sem.at[slot]).start()
        return carry

    lax.fori_loop(0, tm, issue, 0)

    @pl.when(i == n_steps - 1)
    def _():
        if n_steps > 1:
            wait_rows(1 - slot)
        wait_rows(slot)


def _dispatch(x, g, dest, p, tm):
    n, d = x.shape
    n_steps = n // tm
    grid_spec = pltpu.PrefetchScalarGridSpec(
        num_scalar_prefetch=1,
        grid=(n_steps,),
        in_specs=[
            pl.BlockSpec((tm, d), lambda i, dst: (i, 0)),
            pl.BlockSpec((1, d), lambda i, dst: (0, 0)),
            pl.BlockSpec(memory_space=pl.ANY),
        ],
        out_specs=pl.BlockSpec(memory_space=pl.ANY),
        scratch_shapes=[pltpu.VMEM((2, tm, d), F32), pltpu.SemaphoreType.DMA((2,))],
    )
    return pl.pallas_call(
        functools.partial(_dispatch_kernel, tm=tm, n_steps=n_steps),
        grid_spec=grid_spec,
        out_shape=jax.ShapeDtypeStruct((p, d), F32),
        input_output_aliases={3: 0},
        compiler_params=_cparams("arbitrary"),
        name="dispatch",
    )(dest, x, g, jnp.zeros((p, d), F32))


def _combine_kernel(dest_ref, x_ref, info_ref, ys_ref, o_ref, buf, sem, *, tm, n_steps):
    i = pl.program_id(0)
    slot = i % 2

    def issue_step(step, s):
        def issue(r, carry):
            a = (step * tm + r) * EXPERT_TOPK
            for k in range(EXPERT_TOPK):
                _row_copy(ys_ref, dest_ref[a + k], buf.at[s, k], r, sem.at[s]).start()
            return carry
        lax.fori_loop(0, tm, issue, 0)

    @pl.when(i == 0)
    def _():
        issue_step(0, 0)

    @pl.when(i + 1 < n_steps)
    def _():
        issue_step(i + 1, 1 - slot)

    for k in range(EXPERT_TOPK):
        pltpu.make_async_copy(ys_ref.at[pl.ds(0, tm), :], buf.at[slot, k], sem.at[slot]).wait()
    out = x_ref[...]
    for k in range(EXPERT_TOPK):
        out = out + info_ref[:, ROUTE_W + k:ROUTE_W + k + 1] * buf[slot, k]
    o_ref[...] = out


def _combine(x, info, ys, dest, tm):
    n, d = x.shape
    n_steps = n // tm
    grid_spec = pltpu.PrefetchScalarGridSpec(
        num_scalar_prefetch=1,
        grid=(n_steps,),
        in_specs=[
            pl.BlockSpec((tm, d), lambda i, dst: (i, 0)),
            pl.BlockSpec((tm, LANE), lambda i, dst: (i, 0)),
            pl.BlockSpec(memory_space=pl.ANY),
        ],
        out_specs=pl.BlockSpec((tm, d), lambda i, dst: (i, 0)),
        scratch_shapes=[pltpu.VMEM((2, EXPERT_TOPK, tm, d), F32), pltpu.SemaphoreType.DMA((2,))],
    )
    return pl.pallas_call(
        functools.partial(_combine_kernel, tm=tm, n_steps=n_steps),
        grid_spec=grid_spec,
        out_shape=jax.ShapeDtypeStruct((n, d), F32),
        compiler_params=_cparams("arbitrary"),
        name="combine",
    )(dest, x, info, ys)


def _experts_kernel(be_ref, nu_ref, xs_ref, wg_ref, wu_ref, wd_ref, o_ref, wgb, wub, wdb):
    i = pl.program_id(0)
    e = be_ref[i]
    prev = be_ref[jnp.maximum(i - 1, 0)]

    @pl.when((i == 0) | (e != prev))
    def _():
        wgb[...] = wg_ref[...].astype(BF16)
        wub[...] = wu_ref[...].astype(BF16)
        wdb[...] = wd_ref[...].astype(BF16)

    @pl.when(i < nu_ref[0])
    def _():
        xb = xs_ref[...].astype(BF16)
        h1 = jnp.dot(xb, wgb[...], preferred_element_type=F32)
        h2 = jnp.dot(xb, wub[...], preferred_element_type=F32)
        h = h1 * jax.nn.sigmoid(h1) * h2
        o_ref[...] = jnp.dot(h.astype(BF16), wdb[...], preferred_element_type=F32)

    @pl.when(i >= nu_ref[0])
    def _():
        o_ref[...] = jnp.zeros(o_ref.shape, F32)


def _experts(xs, blk_expert, n_used, wg, wu, wd, layer, blk):
    p, d = xs.shape
    de = wg.shape[-1]
    n_blk = p // blk
    grid_spec = pltpu.PrefetchScalarGridSpec(
        num_scalar_prefetch=2,
        grid=(n_blk,),
        in_specs=[
            pl.BlockSpec((blk, d), lambda i, be, nu: (jnp.minimum(i, nu[0] - 1), 0)),
            pl.BlockSpec((None, None, d, de), lambda i, be, nu: (layer, be[i], 0, 0)),
            pl.BlockSpec((None, None, d, de), lambda i, be, nu: (layer, be[i], 0, 0)),
            pl.BlockSpec((None, None, de, d), lambda i, be, nu: (layer, be[i], 0, 0)),
        ],
        out_specs=pl.BlockSpec((blk, d), lambda i, be, nu: (i, 0)),
        scratch_shapes=[pltpu.VMEM((d, de), BF16), pltpu.VMEM((d, de), BF16), pltpu.VMEM((de, d), BF16)],
    )
    return pl.pallas_call(
        _experts_kernel,
        grid_spec=grid_spec,
        out_shape=jax.ShapeDtypeStruct((p, d), F32),
        compiler_params=_cparams("arbitrary"),
        name="experts",
    )(blk_expert, n_used, xs, wg, wu, wd)


def _moe(x, layer, g, w_rg, b_rg, w_re, b_re, wg, wu, wd, *, tm, tm_rows, blk):
    n, d = x.shape
    n_exp = wg.shape[1]
    pad = LANE - N_GROUPS - n_exp
    wr = jnp.concatenate([w_rg, w_re, jnp.zeros((d, pad), F32)], axis=1)
    br = jnp.concatenate([b_rg, b_re, jnp.zeros((pad,), F32)])[None, :]
    info, cnt = _router(x, g, wr, br, tm)

    a = n * EXPERT_TOPK
    experts = jnp.arange(n_exp, dtype=I32)
    counts = cnt[0, :n_exp].astype(I32)
    padded = ((counts + blk - 1) // blk) * blk
    pad_end = jnp.cumsum(padded)
    pad_start = pad_end - padded
    eid = info[:, ROUTE_EID:ROUTE_EID + EXPERT_TOPK].astype(I32)
    rank = info[:, ROUTE_RANK:ROUTE_RANK + EXPERT_TOPK].astype(I32)
    start = jnp.sum(jnp.where(eid[..., None] == experts, pad_start, 0), axis=-1)
    dest = (start + rank).reshape(a)
    n_blk = -(-a // blk) + n_exp
    n_used = (pad_end[-1] // blk).astype(I32)
    blk_ids = jnp.minimum(jnp.arange(n_blk, dtype=I32), n_used - 1)
    blk_expert = jnp.sum((blk_ids[:, None] * blk >= pad_end[None, :]).astype(I32), axis=1)
    blk_expert = jnp.minimum(blk_expert, n_exp - 1)
    xs = _dispatch(x, g, dest, n_blk * blk, tm_rows)
    ys = _experts(xs, blk_expert, n_used.reshape(1), wg, wu, wd, layer, blk)
    return _combine(x, info, ys, dest, tm_rows)


def _head_rms(x, g):
    return x * lax.rsqrt(jnp.mean(x * x, axis=-1, keepdims=True) + EPS) * g


def _kvq_prompt_kernel(x_ref, gkv_ref, gq_ref, wkv_ref, wq_ref, kn_ref, qn_ref,
                       k_ref, v_ref, kb_ref, vt_ref, qt_ref, bias_ref, km_ref, *, n_heads, nb):
    i = pl.program_id(1)
    d = n_heads * HEAD_DIM

    @pl.when(i == 0)
    def _():
        km_ref[...] = jnp.zeros(km_ref.shape, F32)

    x = x_ref[...]
    kv = _dot(_rms(x, gkv_ref[...]), wkv_ref[...])
    q = _dot(_rms(x, gq_ref[...]), wq_ref[...])
    v_ref[...] = kv[:, d:]
    tq = x.shape[0]
    blk_iota = lax.broadcasted_iota(I32, (nb, tq), 0)
    row_iota = lax.broadcasted_iota(I32, (nb, HEAD_DIM), 0)
    ninf = jnp.float32(-jnp.inf)
    for h in range(n_heads):
        hs = slice(h * HEAD_DIM, (h + 1) * HEAD_DIM)
        kh = _head_rms(kv[:, hs], kn_ref[...])
        k_ref[:, hs] = kh
        kb_ref[h] = kh.astype(BF16)
        vt_ref[h] = kv[:, d + h * HEAD_DIM:d + (h + 1) * HEAD_DIM].T.astype(BF16)
        qt = _head_rms(q[:, hs], qn_ref[...]).T
        qt_ref[h] = (qt * QK_SCALE_LOG2).astype(BF16)
        kmh = km_ref[:, hs]
        gate = _dot(kmh, qt)
        gate = jnp.where(blk_iota < i, gate, ninf)
        sel = jnp.zeros((nb, tq), jnp.bool_)
        for _ in range(MOBA_TOPK):
            m = jnp.max(gate, axis=0, keepdims=True)
            idx = jnp.min(jnp.where(gate == m, blk_iota, BIG_IDX), axis=0, keepdims=True)
            hit = blk_iota == idx
            sel = sel | (hit & (m > ninf))
            gate = jnp.where(hit, ninf, gate)
        bias_ref[h] = jnp.where(sel, 0.0, NEG)
        kmean = jnp.mean(kh, axis=0, keepdims=True)
        km_ref[:, hs] = jnp.where(row_iota == i, kmean, kmh)


def _kvq_prompt(x, gkv, gq, w_kv, w_q, kn, qn):
    b, t, d = x.shape
    n_heads = d // HEAD_DIM
    tq = MOBA_BLOCK
    nb = t // tq
    kern = functools.partial(_kvq_prompt_kernel, n_heads=n_heads, nb=nb)
    const = lambda i, j: (0, 0)
    return pl.pallas_call(
        kern,
        grid=(b, nb),
        in_specs=[
            pl.BlockSpec((None, tq, d), lambda i, j: (i, j, 0)),
            pl.BlockSpec((1, d), const),
            pl.BlockSpec((1, d), const),
            pl.BlockSpec((d, 2 * d), const),
            pl.BlockSpec((d, d), const),
            pl.BlockSpec((1, HEAD_DIM), const),
            pl.BlockSpec((1, HEAD_DIM), const),
        ],
        out_specs=[
            pl.BlockSpec((None, tq, d), lambda i, j: (i, j, 0)),
            pl.BlockSpec((None, tq, d), lambda i, j: (i, j, 0)),
            pl.BlockSpec((None, n_heads, tq, HEAD_DIM), lambda i, j: (i, 0, j, 0)),
            pl.BlockSpec((None, n_heads, None, HEAD_DIM, tq), lambda i, j: (i, 0, j, 0, 0)),
            pl.BlockSpec((None, n_heads, None, HEAD_DIM, tq), lambda i, j: (i, 0, j, 0, 0)),
            pl.BlockSpec((None, n_heads, None, nb, tq), lambda i, j: (i, 0, j, 0, 0)),
        ],
        out_shape=[
            jax.ShapeDtypeStruct((b, t, d), F32),
            jax.ShapeDtypeStruct((b, t, d), F32),
            jax.ShapeDtypeStruct((b, n_heads, t, HEAD_DIM), BF16),
            jax.ShapeDtypeStruct((b, n_heads, nb, HEAD_DIM, tq), BF16),
            jax.ShapeDtypeStruct((b, n_heads, nb, HEAD_DIM, tq), BF16),
            jax.ShapeDtypeStruct((b, n_heads, nb, nb, tq), F32),
        ],
        scratch_shapes=[pltpu.VMEM((nb, d), F32)],
        compiler_params=_cparams("arbitrary", "arbitrary"),
        name="kvq_prompt",
    )(x, gkv, gq, w_kv.astype(BF16), w_q.astype(BF16), kn, qn)


def _moba_prompt_kernel(qt_ref, bias_ref, kb_ref, vt_ref, o_ref, *, tq, grp, sub):
    i = pl.program_id(2)
    qt = qt_ref[...]

    def scores(n):
        off = pl.multiple_of(n * tq, tq)
        return jnp.dot(kb_ref[pl.ds(off, tq), :], qt, preferred_element_type=F32)

    def update(blocks, m, l, acc):
        m_new = m
        for s, _ in blocks:
            m_new = jnp.maximum(m_new, jnp.max(s, axis=0, keepdims=True))
        alpha = jnp.exp2(m - m_new)
        l = alpha * l
        acc = alpha * acc
        for s, n in blocks:
            p = jnp.exp2(s - m_new)
            l = l + jnp.sum(p, axis=0, keepdims=True)
            acc = acc + jnp.dot(vt_ref[n], p.astype(BF16), preferred_element_type=F32)
        return m_new, l, acc

    s = scores(i)
    key_i = lax.broadcasted_iota(I32, s.shape, 0)
    qry_i = lax.broadcasted_iota(I32, s.shape, 1)
    blocks = [(jnp.where(key_i <= qry_i, s, NEG), i)]
    for c in range(grp - 1):
        blocks.append((scores(c) + bias_ref[c:c + 1, :], c))
    def update_all(blocks, carry):
        for c in range(0, grp, sub):
            carry = update(blocks[c:c + sub], *carry)
        return carry

    carry = update_all(blocks, (jnp.full((1, tq), NEG, F32), jnp.zeros((1, tq), F32),
                                jnp.zeros((HEAD_DIM, tq), F32)))

    def body(g, carry):
        base = (grp - 1) + g * grp
        blocks = []
        for c in range(grp):
            n = base + c
            blocks.append((scores(n) + bias_ref[pl.ds(n, 1), :], n))
        return update_all(blocks, carry)

    n_groups = (jnp.maximum(i - (grp - 1), 0) + grp - 1) // grp
    _, l, acc = lax.fori_loop(0, n_groups, body, carry)
    o_ref[...] = (acc / l).T.astype(o_ref.dtype)


def _moba_prompt(qt, bias, kb, vt, grp=4, sub=1):
    b, n_heads, nb, _, tq = qt.shape
    t = nb * tq
    assert nb % grp == 0
    return pl.pallas_call(
        functools.partial(_moba_prompt_kernel, tq=tq, grp=grp, sub=sub),
        grid=(b, n_heads, nb),
        in_specs=[
            pl.BlockSpec((None, None, None, HEAD_DIM, tq), lambda bi, h, i: (bi, h, i, 0, 0)),
            pl.BlockSpec((None, None, None, nb, tq), lambda bi, h, i: (bi, h, i, 0, 0)),
            pl.BlockSpec((None, None, t, HEAD_DIM), lambda bi, h, i: (bi, h, 0, 0)),
            pl.BlockSpec((None, None, nb, HEAD_DIM, tq), lambda bi, h, i: (bi, h, 0, 0, 0)),
        ],
        out_specs=pl.BlockSpec((None, tq, HEAD_DIM), lambda bi, h, i: (bi, i, h)),
        out_shape=jax.ShapeDtypeStruct((b, t, n_heads * HEAD_DIM), BF16),
        compiler_params=_cparams("arbitrary", "arbitrary", "arbitrary"),
        name="moba_prompt",
    )(qt, bias, kb, vt)


def _proj_res_kernel(x_ref, a_ref, w_ref, o_ref):
    o_ref[...] = x_ref[...] + _dot(a_ref[...], w_ref[...])


def _proj_res(x, a, w, tm):
    n, d = x.shape
    w = w.astype(BF16)
    return pl.pallas_call(
        _proj_res_kernel,
        grid=(n // tm,),
        in_specs=[
            pl.BlockSpec((tm, d), lambda i: (i, 0)),
            pl.BlockSpec((tm, a.shape[1]), lambda i: (i, 0)),
            pl.BlockSpec(w.shape, lambda i: (0, 0)),
        ],
        out_specs=pl.BlockSpec((tm, d), lambda i: (i, 0)),
        out_shape=jax.ShapeDtypeStruct((n, d), F32),
        compiler_params=_cparams("arbitrary"),
        name="proj_res",
    )(x, a, w)


def _kvq_sample_kernel(x_ref, gkv_ref, gq_ref, wkv_ref, wq_ref, kn_ref, qn_ref, k_ref, v_ref, q_ref,
                       *, n_heads):
    d = n_heads * HEAD_DIM
    x = x_ref[...]
    kv = _dot(_rms(x, gkv_ref[...]), wkv_ref[...])
    q = _dot(_rms(x, gq_ref[...]), wq_ref[...])
    v_ref[...] = kv[:, d:]
    for h in range(n_heads):
        hs = slice(h * HEAD_DIM, (h + 1) * HEAD_DIM)
        k_ref[:, hs] = _head_rms(kv[:, hs], kn_ref[...])
        q_ref[:, hs] = _head_rms(q[:, hs], qn_ref[...])


def _kvq_sample(x, gkv, gq, w_kv, w_q, kn, qn):
    n, d = x.shape
    sds = jax.ShapeDtypeStruct((n, d), F32)
    return pl.pallas_call(
        functools.partial(_kvq_sample_kernel, n_heads=d // HEAD_DIM),
        out_shape=[sds, sds, sds],
        compiler_params=pltpu.CompilerParams(vmem_limit_bytes=VMEM_LIMIT),
        name="kvq_sample",
    )(x, gkv, gq, w_kv, w_q, kn, qn)


def _sample_blocks_kernel(pt_ref, q_ref, *refs, pages_per_step, pages_per_block):
    del pt_ref
    k_refs = refs[:pages_per_step]
    v_refs = refs[pages_per_step:2 * pages_per_step]
    km_ref, m_ref, l_ref, o_ref = refs[2 * pages_per_step:]
    q = q_ref[...] * QK_SCALE
    n_keys = pages_per_block * k_refs[0].shape[0]
    for c in range(pages_per_step // pages_per_block):
        pages = range(c * pages_per_block, (c + 1) * pages_per_block)
        ks = [k_refs[r][...] for r in pages]
        s = [jnp.sum(kp * q[None], axis=-1, keepdims=True) for kp in ks]
        m = s[0].max(axis=0)
        ksum = ks[0].sum(axis=0)
        for sr, kp in zip(s[1:], ks[1:]):
            m = jnp.maximum(m, sr.max(axis=0))
            ksum = ksum + kp.sum(axis=0)
        l = jnp.zeros_like(m)
        o = jnp.zeros(q.shape, F32)
        for sr, r in zip(s, pages):
            p = jnp.exp(sr - m[None])
            l = l + p.sum(axis=0)
            o = o + (p * v_refs[r][...]).sum(axis=0)
        km_ref[c] = ksum * (1.0 / n_keys)
        m_ref[c] = jnp.broadcast_to(m, q.shape)
        l_ref[c] = jnp.broadcast_to(l, q.shape)
        o_ref[c] = o


def _sample_blocks(q, cache_k, cache_v, page_table, pages_per_step=8):
    n_seq, n_pages = page_table.shape
    _, page, n_heads, hd = cache_k.shape
    ppb = MOBA_BLOCK // page
    assert MOBA_BLOCK % page == 0 and n_pages % pages_per_step == 0 and pages_per_step % ppb == 0
    steps = n_pages // pages_per_step
    bps = pages_per_step // ppb

    def page_spec(c):
        return pl.BlockSpec((None, page, n_heads, hd),
                            lambda s, g, pt: (pt[(s * steps + g) * pages_per_step + c], 0, 0, 0))

    blk_out = pl.BlockSpec((None, bps, n_heads, hd), lambda s, g, pt: (s, g, 0, 0))
    grid_spec = pltpu.PrefetchScalarGridSpec(
        num_scalar_prefetch=1,
        grid=(n_seq, steps),
        in_specs=[pl.BlockSpec((None, n_heads, hd), lambda s, g, pt: (s, 0, 0))]
        + [page_spec(c) for c in range(pages_per_step)] * 2,
        out_specs=[blk_out] * 4,
    )
    sds = jax.ShapeDtypeStruct((n_seq, n_pages // ppb, n_heads, hd), F32)
    return pl.pallas_call(
        functools.partial(_sample_blocks_kernel, pages_per_step=pages_per_step, pages_per_block=ppb),
        grid_spec=grid_spec,
        out_shape=[sds] * 4,
        compiler_params=_cparams("arbitrary", "arbitrary"),
        name="sample_blocks",
    )(page_table.reshape(-1), q, *([cache_k] * pages_per_step), *([cache_v] * pages_per_step))


def _sample_combine_kernel(q_ref, kn_ref, vn_ref, km_ref, m_ref, l_ref, o_ref, out_ref):
    q = q_ref[...]
    qb = q.astype(BF16).astype(F32)
    gate = jnp.sum(km_ref[...].astype(BF16).astype(F32) * qb[None], axis=-1, keepdims=True)
    blk_iota = lax.broadcasted_iota(I32, gate.shape, 0)
    ninf = jnp.float32(-jnp.inf)
    sel = jnp.zeros(gate.shape, jnp.bool_)
    for _ in range(MOBA_TOPK):
        g_max = jnp.max(gate, axis=0, keepdims=True)
        idx = jnp.min(jnp.where(gate == g_max, blk_iota, BIG_IDX), axis=0, keepdims=True)
        hit = blk_iota == idx
        sel = sel | (hit & (g_max > ninf))
        gate = jnp.where(hit, ninf, gate)
    s_new = jnp.sum(q * QK_SCALE * kn_ref[...], axis=-1, keepdims=True)
    mb = m_ref[...]
    m_tot = jnp.maximum(jnp.max(jnp.where(sel, mb, ninf), axis=0), s_new)
    w = jnp.where(sel, jnp.exp(mb - m_tot[None]), 0.0)
    p_new = jnp.exp(s_new - m_tot)
    l = jnp.sum(w * l_ref[...], axis=0) + p_new
    o = jnp.sum(w * o_ref[...], axis=0) + p_new * vn_ref[...]
    out_ref[...] = o / l


def _sample_combine(q, k_new, v_new, km, mb, lb, ob):
    n_seq, nblk, n_heads, hd = km.shape
    row = pl.BlockSpec((None, n_heads, hd), lambda s: (s, 0, 0))
    blk = pl.BlockSpec((None, nblk, n_heads, hd), lambda s: (s, 0, 0, 0))
    return pl.pallas_call(
        _sample_combine_kernel,
        grid=(n_seq,),
        in_specs=[row, row, row, blk, blk, blk, blk],
        out_specs=row,
        out_shape=jax.ShapeDtypeStruct((n_seq, n_heads, hd), F32),
        compiler_params=_cparams("arbitrary"),
        name="sample_combine",
    )(q, k_new, v_new, km, mb, lb, ob)


def kernel(x_prompt, x_sample, state_conv, cache_k, cache_v, page_table, norm_mix, norm_ffn, norm_kv,
           w_conv_in, w_conv, w_conv_out, w_kv, k_norm, w_q, q_norm, w_o, w_router_group,
           b_router_group, w_router_expert, b_router_expert, w_expert_gate, w_expert_up, w_expert_down):
    bp, tp, d = x_prompt.shape
    bs = x_sample.shape[0]
    n_heads = d // HEAD_DIM
    kn = k_norm[None, :]
    qn = q_norm[0][None, :]

    def moe(x, layer, tm, tm_rows, blk):
        return _moe(x, layer, norm_ffn[layer][None, :], w_router_group[layer], b_router_group[layer],
                    w_router_expert[layer], b_router_expert[layer],
                    w_expert_gate, w_expert_up, w_expert_down, tm=tm, tm_rows=tm_rows, blk=blk)

    x1, conv_p = _mixer_prompt(x_prompt, norm_mix[0][None, :], w_conv_in[0], w_conv[0], w_conv_out[0])
    x2 = moe(x1.reshape(bp * tp, d), 0, 512, 256, 256)
    k_p, v_p, kb, vt, qt, bias = _kvq_prompt(x2.reshape(bp, tp, d), norm_kv[None, :], norm_mix[1][None, :],
                                             w_kv, w_q[0], kn, qn)
    attn = _moba_prompt(qt, bias, kb, vt)
    x3 = _proj_res(x2, attn.reshape(bp * tp, d), w_o[0], 512)
    y_p = moe(x3, 1, 512, 256, 256).reshape(bp, tp, d)

    xs = x_sample.reshape(bs, d)
    st = state_conv[0]
    x1s, up_s = _mixer_sample(xs, st[:, 0, :], st[:, 1, :], norm_mix[0][None, :],
                              w_conv_in[0], w_conv[0], w_conv_out[0])
    conv_s = jnp.stack([st[:, 1, :], up_s], axis=1)[None]
    x2s = moe(x1s, 0, bs, bs, 16)
    k_s, v_s, q_s = _kvq_sample(x2s, norm_kv[None, :], norm_mix[1][None, :], w_kv, w_q[0], kn, qn)
    heads = lambda z: z.reshape(bs, n_heads, HEAD_DIM)
    km, mb, lb, ob = _sample_blocks(heads(q_s), cache_k, cache_v, page_table)
    attn_s = _sample_combine(heads(q_s), heads(k_s), heads(v_s), km, mb, lb, ob)
    x3s = _proj_res(x2s, attn_s.reshape(bs, d), w_o[0], bs)
    y_s = moe(x3s, 1, bs, bs, 16).reshape(bs, 1, d)

    return (y_p, y_s, conv_p[None],
            k_p.reshape(bp, tp, n_heads, HEAD_DIM), v_p.reshape(bp, tp, n_heads, HEAD_DIM),
            conv_s,
            k_s.reshape(bs, 1, n_heads, HEAD_DIM), v_s.reshape(bs, 1, n_heads, HEAD_DIM))
```

```python
import functools

import jax
import jax.numpy as jnp
from jax import lax
from jax.experimental import pallas as pl
from jax.experimental.pallas import tpu as pltpu

F32 = jnp.float32
BF16 = jnp.bfloat16
I32 = jnp.int32

EPS = 1e-6
HEAD_DIM = 128
MOBA_BLOCK = 256
MOBA_TOPK = 3
N_GROUPS = 4
EXPERTS_PER_GROUP = 8
EXPERT_TOPK = 2
CONV_W = 3

LANE = 128
SUBLANE = 8
VMEM_LIMIT = 48 * 1024 * 1024
NEG = -1e30
BIG_IDX = 1 << 20
LOG2_E = 1.4426950408889634
QK_SCALE = HEAD_DIM ** -0.5
QK_SCALE_LOG2 = QK_SCALE * LOG2_E


def _cparams(*sem):
    return pltpu.CompilerParams(dimension_semantics=sem, vmem_limit_bytes=VMEM_LIMIT)


def _rms(x, g):
    ms = jnp.mean(x * x, axis=-1, keepdims=True)
    return x * lax.rsqrt(ms + EPS) * g


def _dot(a, b):
    return jnp.dot(a.astype(BF16), b.astype(BF16), preferred_element_type=F32)


def _mixer_prompt_kernel(x_ref, g_ref, win_ref, wc_ref, wout_ref, o_ref, st_ref, buf_ref, *, tm, d):
    @pl.when(pl.program_id(1) == 0)
    def _():
        buf_ref[0:SUBLANE, :] = jnp.zeros((SUBLANE, d), F32)

    x = x_ref[...]
    h = _rms(x, g_ref[...])
    u = _dot(h, win_ref[...])
    bg = u[:, :d]
    up = u[:, d:2 * d] * u[:, 2 * d:]
    buf_ref[SUBLANE:SUBLANE + tm, :] = up
    up1 = buf_ref[SUBLANE - 1:SUBLANE - 1 + tm, :]
    up2 = buf_ref[SUBLANE - 2:SUBLANE - 2 + tm, :]
    wc = wc_ref[...]
    y = wc[0:1] * up2 + wc[1:2] * up1 + wc[2:3] * up
    o_ref[...] = x + _dot(bg * y, wout_ref[...])
    last2 = buf_ref[tm + SUBLANE - 2:tm + SUBLANE, :]
    st_ref[...] = last2
    buf_ref[SUBLANE - 2:SUBLANE, :] = last2


def _mixer_prompt(x, g, w_in, w_c, w_out, tm=256):
    b, t, d = x.shape
    kern = functools.partial(_mixer_prompt_kernel, tm=tm, d=d)
    return pl.pallas_call(
        kern,
        grid=(b, t // tm),
        in_specs=[
            pl.BlockSpec((None, tm, d), lambda i, j: (i, j, 0)),
            pl.BlockSpec((1, d), lambda i, j: (0, 0)),
            pl.BlockSpec((d, 3 * d), lambda i, j: (0, 0)),
            pl.BlockSpec((CONV_W, d), lambda i, j: (0, 0)),
            pl.BlockSpec((d, d), lambda i, j: (0, 0)),
        ],
        out_specs=[
            pl.BlockSpec((None, tm, d), lambda i, j: (i, j, 0)),
            pl.BlockSpec((None, CONV_W - 1, d), lambda i, j: (i, 0, 0)),
        ],
        out_shape=[
            jax.ShapeDtypeStruct((b, t, d), F32),
            jax.ShapeDtypeStruct((b, CONV_W - 1, d), F32),
        ],
        scratch_shapes=[pltpu.VMEM((tm + SUBLANE, d), F32)],
        compiler_params=_cparams("arbitrary", "arbitrary"),
        name="mixer_prompt",
    )(x, g, w_in.astype(BF16), w_c, w_out.astype(BF16))


def _mixer_sample_kernel(x_ref, p0_ref, p1_ref, g_ref, win_ref, wc_ref, wout_ref, o_ref, up_ref, *, d):
    x = x_ref[...]
    h = _rms(x, g_ref[...])
    u = _dot(h, win_ref[...])
    bg = u[:, :d]
    up = u[:, d:2 * d] * u[:, 2 * d:]
    wc = wc_ref[...]
    y = wc[0:1] * p0_ref[...] + wc[1:2] * p1_ref[...] + wc[2:3] * up
    o_ref[...] = x + _dot(bg * y, wout_ref[...])
    up_ref[...] = up


def _mixer_sample(x, p0, p1, g, w_in, w_c, w_out):
    n, d = x.shape
    return pl.pallas_call(
        functools.partial(_mixer_sample_kernel, d=d),
        out_shape=[jax.ShapeDtypeStruct((n, d), F32), jax.ShapeDtypeStruct((n, d), F32)],
        compiler_params=pltpu.CompilerParams(vmem_limit_bytes=VMEM_LIMIT),
        name="mixer_sample",
    )(x, p0, p1, g, w_in, w_c, w_out)


def _router_kernel(x_ref, g_ref, wr_ref, br_ref, info_ref, cnt_ref, run_ref):
    @pl.when(pl.program_id(0) == 0)
    def _():
        run_ref[...] = jnp.zeros(run_ref.shape, F32)

    xn = _rms(x_ref[...], g_ref[...])
    lg = _dot(xn, wr_ref[...]) + br_ref[...]
    lane = lax.broadcasted_iota(I32, lg.shape, 1)
    ninf = jnp.float32(-jnp.inf)
    is_g = lane < N_GROUPS
    gl = jnp.where(is_g, lg, ninf)
    gmax = jnp.max(gl, axis=-1, keepdims=True)
    gsel = jnp.min(jnp.where(gl == gmax, lane, BIG_IDX), axis=-1, keepdims=True)
    gsum = jnp.sum(jnp.where(is_g, jnp.exp(gl - gmax), 0.0), axis=-1, keepdims=True)
    p_group = 1.0 / gsum
    lo = N_GROUPS + gsel * EXPERTS_PER_GROUP
    in_grp = (lane >= lo) & (lane < lo + EXPERTS_PER_GROUP)
    el = jnp.where(in_grp, lg, ninf)
    e1 = jnp.max(el, axis=-1, keepdims=True)
    i1 = jnp.min(jnp.where(el == e1, lane, BIG_IDX), axis=-1, keepdims=True)
    el2 = jnp.where(lane == i1, ninf, el)
    e2 = jnp.max(el2, axis=-1, keepdims=True)
    i2 = jnp.min(jnp.where(el2 == e2, lane, BIG_IDX), axis=-1, keepdims=True)
    r = jnp.exp(e2 - e1)
    w1 = p_group / (1.0 + r)
    w2 = w1 * r
    eid1 = i1 - N_GROUPS
    eid2 = i2 - N_GROUPS
    hit1 = lane == eid1
    hit2 = lane == eid2
    onehot = jnp.where(hit1 | hit2, 1.0, 0.0)
    tm = onehot.shape[0]
    tri = jnp.where(lax.broadcasted_iota(I32, (tm, tm), 0) > lax.broadcasted_iota(I32, (tm, tm), 1), 1.0, 0.0)
    before = _dot(tri, onehot) + run_ref[0:1, :]
    rank1 = jnp.sum(jnp.where(hit1, before, 0.0), axis=-1, keepdims=True)
    rank2 = jnp.sum(jnp.where(hit2, before, 0.0), axis=-1, keepdims=True)
    total = run_ref[0:1, :] + jnp.sum(onehot, axis=0, keepdims=True)
    run_ref[0:1, :] = total
    cnt_ref[...] = total
    cols = (eid1.astype(F32), eid2.astype(F32), w1, w2, rank1, rank2)
    info = jnp.zeros(lg.shape, F32)
    for c, val in enumerate(cols):
        info = jnp.where(lane == c, val, info)
    info_ref[...] = info


ROUTE_EID, ROUTE_W, ROUTE_RANK = 0, 2, 4


def _router(x, g, wr, br, tm):
    n, d = x.shape
    return pl.pallas_call(
        _router_kernel,
        grid=(n // tm,),
        in_specs=[
            pl.BlockSpec((tm, d), lambda i: (i, 0)),
            pl.BlockSpec((1, d), lambda i: (0, 0)),
            pl.BlockSpec((d, LANE), lambda i: (0, 0)),
            pl.BlockSpec((1, LANE), lambda i: (0, 0)),
        ],
        out_specs=[
            pl.BlockSpec((tm, LANE), lambda i: (i, 0)),
            pl.BlockSpec((1, LANE), lambda i: (0, 0)),
        ],
        out_shape=[jax.ShapeDtypeStruct((n, LANE), F32), jax.ShapeDtypeStruct((1, LANE), F32)],
        scratch_shapes=[pltpu.VMEM((SUBLANE, LANE), F32)],
        compiler_params=_cparams("arbitrary"),
        name="router",
    )(x, g, wr, br)


ISSUE_UNROLL = 8


def _row_copy(src, src_row, dst, dst_row, sem):
    return pltpu.make_async_copy(src.at[pl.ds(src_row, 1), :], dst.at[pl.ds(dst_row, 1), :], sem)


def _dispatch_kernel(dest_ref, x_ref, g_ref, zeros_ref, xs_ref, buf, sem, *, tm, n_steps):
    del zeros_ref
    i = pl.program_id(0)
    slot = i % 2

    def wait_rows(s):
        for _ in range(EXPERT_TOPK):
            pltpu.make_async_copy(buf.at[s], xs_ref.at[pl.ds(0, tm), :], sem.at[s]).wait()

    @pl.when(i >= 2)
    def _():
        wait_rows(slot)

    buf[slot] = _rms(x_ref[...], g_ref[...])

    def issue(r, carry):
        a = (i * tm + r) * EXPERT_TOPK
        for k in range(EXPERT_TOPK):
            _row_copy(buf.at[slot], r, xs_ref, dest_ref[a + k], sem.at[slot]).start()
        return carry

    lax.fori_loop(0, tm, issue, 0, unroll=ISSUE_UNROLL)

    @pl.when(i == n_steps - 1)
    def _():
        if n_steps > 1:
            wait_rows(1 - slot)
        wait_rows(slot)


def _dispatch(x, g, dest, p, tm):
    n, d = x.shape
    n_steps = n // tm
    grid_spec = pltpu.PrefetchScalarGridSpec(
        num_scalar_prefetch=1,
        grid=(n_steps,),
        in_specs=[
            pl.BlockSpec((tm, d), lambda i, dst: (i, 0)),
            pl.BlockSpec((1, d), lambda i, dst: (0, 0)),
            pl.BlockSpec(memory_space=pl.ANY),
        ],
        out_specs=pl.BlockSpec(memory_space=pl.ANY),
        scratch_shapes=[pltpu.VMEM((2, tm, d), F32), pltpu.SemaphoreType.DMA((2,))],
    )
    return pl.pallas_call(
        functools.partial(_dispatch_kernel, tm=tm, n_steps=n_steps),
        grid_spec=grid_spec,
        out_shape=jax.ShapeDtypeStruct((p, d), F32),
        input_output_aliases={3: 0},
        compiler_params=_cparams("arbitrary"),
        name="dispatch",
    )(dest, x, g, jnp.zeros((p, d), F32))


def _combine_kernel(dest_ref, x_ref, info_ref, ys_ref, o_ref, buf, sem, *, tm, n_steps):
    i = pl.program_id(0)
    slot = i % 2

    def issue_step(step, s):
        def issue(r, carry):
            a = (step * tm + r) * EXPERT_TOPK
            for k in range(EXPERT_TOPK):
                _row_copy(ys_ref, dest_ref[a + k], buf.at[s, k], r, sem.at[s]).start()
            return carry
        lax.fori_loop(0, tm, issue, 0, unroll=ISSUE_UNROLL)

    @pl.when(i == 0)
    def _():
        issue_step(0, 0)

    @pl.when(i + 1 < n_steps)
    def _():
        issue_step(i + 1, 1 - slot)

    for k in range(EXPERT_TOPK):
        pltpu.make_async_copy(ys_ref.at[pl.ds(0, tm), :], buf.at[slot, k], sem.at[slot]).wait()
    out = x_ref[...]
    for k in range(EXPERT_TOPK):
        out = out + info_ref[:, ROUTE_W + k:ROUTE_W + k + 1] * buf[slot, k]
    o_ref[...] = out


def _combine(x, info, ys, dest, tm):
    n, d = x.shape
    n_steps = n // tm
    grid_spec = pltpu.PrefetchScalarGridSpec(
        num_scalar_prefetch=1,
        grid=(n_steps,),
        in_specs=[
            pl.BlockSpec((tm, d), lambda i, dst: (i, 0)),
            pl.BlockSpec((tm, LANE), lambda i, dst: (i, 0)),
            pl.BlockSpec(memory_space=pl.ANY),
        ],
        out_specs=pl.BlockSpec((tm, d), lambda i, dst: (i, 0)),
        scratch_shapes=[pltpu.VMEM((2, EXPERT_TOPK, tm, d), F32), pltpu.SemaphoreType.DMA((2,))],
    )
    return pl.pallas_call(
        functools.partial(_combine_kernel, tm=tm, n_steps=n_steps),
        grid_spec=grid_spec,
        out_shape=jax.ShapeDtypeStruct((n, d), F32),
        compiler_params=_cparams("arbitrary"),
        name="combine",
    )(dest, x, info, ys)


def _experts_kernel(be_ref, nu_ref, xs_ref, wg_ref, wu_ref, wd_ref, o_ref, wgb, wub, wdb):
    i = pl.program_id(0)
    e = be_ref[i]
    prev = be_ref[jnp.maximum(i - 1, 0)]

    @pl.when((i == 0) | (e != prev))
    def _():
        wgb[...] = wg_ref[...].astype(BF16)
        wub[...] = wu_ref[...].astype(BF16)
        wdb[...] = wd_ref[...].astype(BF16)

    @pl.when(i < nu_ref[0])
    def _():
        xb = xs_ref[...].astype(BF16)
        h1 = jnp.dot(xb, wgb[...], preferred_element_type=F32)
        h2 = jnp.dot(xb, wub[...], preferred_element_type=F32)
        h = h1 * jax.nn.sigmoid(h1) * h2
        o_ref[...] = jnp.dot(h.astype(BF16), wdb[...], preferred_element_type=F32)

    @pl.when(i >= nu_ref[0])
    def _():
        o_ref[...] = jnp.zeros(o_ref.shape, F32)


def _experts(xs, blk_expert, n_used, wg, wu, wd, layer, blk):
    p, d = xs.shape
    de = wg.shape[-1]
    n_blk = p // blk
    grid_spec = pltpu.PrefetchScalarGridSpec(
        num_scalar_prefetch=2,
        grid=(n_blk,),
        in_specs=[
            pl.BlockSpec((blk, d), lambda i, be, nu: (jnp.minimum(i, nu[0] - 1), 0)),
            pl.BlockSpec((None, None, d, de), lambda i, be, nu: (layer, be[i], 0, 0)),
            pl.BlockSpec((None, None, d, de), lambda i, be, nu: (layer, be[i], 0, 0)),
            pl.BlockSpec((None, None, de, d), lambda i, be, nu: (layer, be[i], 0, 0)),
        ],
        out_specs=pl.BlockSpec((blk, d), lambda i, be, nu: (i, 0)),
        scratch_shapes=[pltpu.VMEM((d, de), BF16), pltpu.VMEM((d, de), BF16), pltpu.VMEM((de, d), BF16)],
    )
    return pl.pallas_call(
        _experts_kernel,
        grid_spec=grid_spec,
        out_shape=jax.ShapeDtypeStruct((p, d), F32),
        compiler_params=_cparams("arbitrary"),
        name="experts",
    )(blk_expert, n_used, xs, wg, wu, wd)


def _moe(x, layer, g, w_rg, b_rg, w_re, b_re, wg, wu, wd, *, tm, tm_rows, blk):
    n, d = x.shape
    n_exp = wg.shape[1]
    pad = LANE - N_GROUPS - n_exp
    wr = jnp.concatenate([w_rg, w_re, jnp.zeros((d, pad), F32)], axis=1)
    br = jnp.concatenate([b_rg, b_re, jnp.zeros((pad,), F32)])[None, :]
    info, cnt = _router(x, g, wr, br, tm)

    a = n * EXPERT_TOPK
    experts = jnp.arange(n_exp, dtype=I32)
    counts = cnt[0, :n_exp].astype(I32)
    padded = ((counts + blk - 1) // blk) * blk
    pad_end = jnp.cumsum(padded)
    pad_start = pad_end - padded
    eid = info[:, ROUTE_EID:ROUTE_EID + EXPERT_TOPK].astype(I32)
    rank = info[:, ROUTE_RANK:ROUTE_RANK + EXPERT_TOPK].astype(I32)
    start = jnp.sum(jnp.where(eid[..., None] == experts, pad_start, 0), axis=-1)
    dest = (start + rank).reshape(a)
    n_blk = -(-a // blk) + n_exp
    n_used = (pad_end[-1] // blk).astype(I32)
    blk_ids = jnp.minimum(jnp.arange(n_blk, dtype=I32), n_used - 1)
    blk_expert = jnp.sum((blk_ids[:, None] * blk >= pad_end[None, :]).astype(I32), axis=1)
    blk_expert = jnp.minimum(blk_expert, n_exp - 1)
    xs = _dispatch(x, g, dest, n_blk * blk, tm_rows)
    ys = _experts(xs, blk_expert, n_used.reshape(1), wg, wu, wd, layer, blk)
    return _combine(x, info, ys, dest, tm_rows)


def _head_rms(x, g):
    return x * lax.rsqrt(jnp.mean(x * x, axis=-1, keepdims=True) + EPS) * g


def _kvq_prompt_kernel(x_ref, gkv_ref, gq_ref, wkv_ref, wq_ref, kn_ref, qn_ref,
                       k_ref, v_ref, kb_ref, vt_ref, qt_ref, bias_ref, km_ref, *, n_heads, nb):
    i = pl.program_id(1)
    d = n_heads * HEAD_DIM

    @pl.when(i == 0)
    def _():
        km_ref[...] = jnp.zeros(km_ref.shape, F32)

    x = x_ref[...]
    kv = _dot(_rms(x, gkv_ref[...]), wkv_ref[...])
    q = _dot(_rms(x, gq_ref[...]), wq_ref[...])
    v_ref[...] = kv[:, d:]
    tq = x.shape[0]
    blk_iota = lax.broadcasted_iota(I32, (nb, tq), 0)
    row_iota = lax.broadcasted_iota(I32, (nb, HEAD_DIM), 0)
    ninf = jnp.float32(-jnp.inf)
    for h in range(n_heads):
        hs = slice(h * HEAD_DIM, (h + 1) * HEAD_DIM)
        kh = _head_rms(kv[:, hs], kn_ref[...])
        k_ref[:, hs] = kh
        kb_ref[h] = kh.astype(BF16)
        vt_ref[h] = kv[:, d + h * HEAD_DIM:d + (h + 1) * HEAD_DIM].T.astype(BF16)
        qt = _head_rms(q[:, hs], qn_ref[...]).T
        qt_ref[h] = (qt * QK_SCALE_LOG2).astype(BF16)
        kmh = km_ref[:, hs]
        gate = _dot(kmh, qt)
        gate = jnp.where(blk_iota < i, gate, ninf)
        sel = jnp.zeros((nb, tq), jnp.bool_)
        for _ in range(MOBA_TOPK):
            m = jnp.max(gate, axis=0, keepdims=True)
            idx = jnp.min(jnp.where(gate == m, blk_iota, BIG_IDX), axis=0, keepdims=True)
            hit = blk_iota == idx
            sel = sel | (hit & (m > ninf))
            gate = jnp.where(hit, ninf, gate)
        bias_ref[h] = jnp.where(sel, 0.0, NEG)
        kmean = jnp.mean(kh, axis=0, keepdims=True)
        km_ref[:, hs] = jnp.where(row_iota == i, kmean, kmh)


def _kvq_prompt(x, gkv, gq, w_kv, w_q, kn, qn):
    b, t, d = x.shape
    n_heads = d // HEAD_DIM
    tq = MOBA_BLOCK
    nb = t // tq
    kern = functools.partial(_kvq_prompt_kernel, n_heads=n_heads, nb=nb)
    const = lambda i, j: (0, 0)
    return pl.pallas_call(
        kern,
        grid=(b, nb),
        in_specs=[
            pl.BlockSpec((None, tq, d), lambda i, j: (i, j, 0)),
            pl.BlockSpec((1, d), const),
            pl.BlockSpec((1, d), const),
            pl.BlockSpec((d, 2 * d), const),
            pl.BlockSpec((d, d), const),
            pl.BlockSpec((1, HEAD_DIM), const),
            pl.BlockSpec((1, HEAD_DIM), const),
        ],
        out_specs=[
            pl.BlockSpec((None, tq, d), lambda i, j: (i, j, 0)),
            pl.BlockSpec((None, tq, d), lambda i, j: (i, j, 0)),
            pl.BlockSpec((None, n_heads, tq, HEAD_DIM), lambda i, j: (i, 0, j, 0)),
            pl.BlockSpec((None, n_heads, None, HEAD_DIM, tq), lambda i, j: (i, 0, j, 0, 0)),
            pl.BlockSpec((None, n_heads, None, HEAD_DIM, tq), lambda i, j: (i, 0, j, 0, 0)),
            pl.BlockSpec((None, n_heads, None, nb, tq), lambda i, j: (i, 0, j, 0, 0)),
        ],
        out_shape=[
            jax.ShapeDtypeStruct((b, t, d), F32),
            jax.ShapeDtypeStruct((b, t, d), F32),
            jax.ShapeDtypeStruct((b, n_heads, t, HEAD_DIM), BF16),
            jax.ShapeDtypeStruct((b, n_heads, nb, HEAD_DIM, tq), BF16),
            jax.ShapeDtypeStruct((b, n_heads, nb, HEAD_DIM, tq), BF16),
            jax.ShapeDtypeStruct((b, n_heads, nb, nb, tq), F32),
        ],
        scratch_shapes=[pltpu.VMEM((nb, d), F32)],
        compiler_params=_cparams("arbitrary", "arbitrary"),
        name="kvq_prompt",
    )(x, gkv, gq, w_kv.astype(BF16), w_q.astype(BF16), kn, qn)


def _moba_prompt_kernel(qt_ref, bias_ref, kb_ref, vt_ref, o_ref, *, tq, grp, heads):
    i = pl.program_id(2)
    qts = [qt_ref[h] for h in range(heads)]

    def scores(h, n):
        off = pl.multiple_of(n * tq, tq)
        return jnp.dot(kb_ref[h, pl.ds(off, tq), :], qts[h], preferred_element_type=F32)

    def update(h, s, n, m, l, acc):
        m_new = jnp.maximum(m, jnp.max(s, axis=0, keepdims=True))
        alpha = jnp.exp2(m - m_new)
        p = jnp.exp2(s - m_new)
        l = alpha * l + jnp.sum(p, axis=0, keepdims=True)
        acc = alpha * acc + jnp.dot(vt_ref[h, n], p.astype(BF16), preferred_element_type=F32)
        return m_new, l, acc

    key_i = lax.broadcasted_iota(I32, (tq, tq), 0)
    qry_i = lax.broadcasted_iota(I32, (tq, tq), 1)
    def run_group(blocks, carries):
        s = [[(scores(h, n) + bias(h)) if bias is not None else
              jnp.where(key_i <= qry_i, scores(h, n), NEG) for n, bias in blocks] for h in range(heads)]
        carries = list(carries)
        for c, (n, _) in enumerate(blocks):
            for h in range(heads):
                carries[h] = update(h, s[h][c], n, *carries[h])
        return tuple(carries)

    init = (jnp.full((1, tq), NEG, F32), jnp.zeros((1, tq), F32), jnp.zeros((HEAD_DIM, tq), F32))
    first = [(i, None)] + [(c, functools.partial(lambda h, c: bias_ref[h, c:c + 1, :], c=c))
                           for c in range(grp - 1)]
    carries = run_group(first, (init,) * heads)

    def body(g, carries):
        base = (grp - 1) + g * grp
        blocks = [(base + c, functools.partial(lambda h, n: bias_ref[h, pl.ds(n, 1), :], n=base + c))
                  for c in range(grp)]
        return run_group(blocks, carries)

    n_groups = (jnp.maximum(i - (grp - 1), 0) + grp - 1) // grp
    carries = lax.fori_loop(0, n_groups, body, tuple(carries))
    for h in range(heads):
        _, l, acc = carries[h]
        o_ref[:, h * HEAD_DIM:(h + 1) * HEAD_DIM] = (acc / l).T.astype(o_ref.dtype)


def _moba_prompt(qt, bias, kb, vt, grp=2, heads=4):
    b, n_heads, nb, _, tq = qt.shape
    t = nb * tq
    assert nb % grp == 0
    assert n_heads % heads == 0
    return pl.pallas_call(
        functools.partial(_moba_prompt_kernel, tq=tq, grp=grp, heads=heads),
        grid=(b, n_heads // heads, nb),
        in_specs=[
            pl.BlockSpec((None, heads, None, HEAD_DIM, tq), lambda bi, h, i: (bi, h, i, 0, 0)),
            pl.BlockSpec((None, heads, None, nb, tq), lambda bi, h, i: (bi, h, i, 0, 0)),
            pl.BlockSpec((None, heads, t, HEAD_DIM), lambda bi, h, i: (bi, h, 0, 0)),
            pl.BlockSpec((None, heads, nb, HEAD_DIM, tq), lambda bi, h, i: (bi, h, 0, 0, 0)),
        ],
        out_specs=pl.BlockSpec((None, tq, heads * HEAD_DIM), lambda bi, h, i: (bi, i, h)),
        out_shape=jax.ShapeDtypeStruct((b, t, n_heads * HEAD_DIM), BF16),
        compiler_params=_cparams("arbitrary", "arbitrary", "arbitrary"),
        name="moba_prompt",
    )(qt, bias, kb, vt)


def _proj_res_kernel(x_ref, a_ref, w_ref, o_ref):
    o_ref[...] = x_ref[...] + _dot(a_ref[...], w_ref[...])


def _proj_res(x, a, w, tm):
    n, d = x.shape
    w = w.astype(BF16)
    return pl.pallas_call(
        _proj_res_kernel,
        grid=(n // tm,),
        in_specs=[
            pl.BlockSpec((tm, d), lambda i: (i, 0)),
            pl.BlockSpec((tm, a.shape[1]), lambda i: (i, 0)),
            pl.BlockSpec(w.shape, lambda i: (0, 0)),
        ],
        out_specs=pl.BlockSpec((tm, d), lambda i: (i, 0)),
        out_shape=jax.ShapeDtypeStruct((n, d), F32),
        compiler_params=_cparams("arbitrary"),
        name="proj_res",
    )(x, a, w)


def _kvq_sample_kernel(x_ref, gkv_ref, gq_ref, wkv_ref, wq_ref, kn_ref, qn_ref, k_ref, v_ref, q_ref,
                       *, n_heads):
    d = n_heads * HEAD_DIM
    x = x_ref[...]
    kv = _dot(_rms(x, gkv_ref[...]), wkv_ref[...])
    q = _dot(_rms(x, gq_ref[...]), wq_ref[...])
    v_ref[...] = kv[:, d:]
    for h in range(n_heads):
        hs = slice(h * HEAD_DIM, (h + 1) * HEAD_DIM)
        k_ref[:, hs] = _head_rms(kv[:, hs], kn_ref[...])
        q_ref[:, hs] = _head_rms(q[:, hs], qn_ref[...])


def _kvq_sample(x, gkv, gq, w_kv, w_q, kn, qn):
    n, d = x.shape
    sds = jax.ShapeDtypeStruct((n, d), F32)
    return pl.pallas_call(
        functools.partial(_kvq_sample_kernel, n_heads=d // HEAD_DIM),
        out_shape=[sds, sds, sds],
        compiler_params=pltpu.CompilerParams(vmem_limit_bytes=VMEM_LIMIT),
        name="kvq_sample",
    )(x, gkv, gq, w_kv, w_q, kn, qn)


def _sample_blocks_kernel(pt_ref, q_ref, *refs, pages_per_step, pages_per_block):
    del pt_ref
    k_refs = refs[:pages_per_step]
    v_refs = refs[pages_per_step:2 * pages_per_step]
    km_ref, m_ref, l_ref, o_ref = refs[2 * pages_per_step:]
    q = q_ref[...] * QK_SCALE
    n_keys = pages_per_block * k_refs[0].shape[0]
    for c in range(pages_per_step // pages_per_block):
        pages = range(c * pages_per_block, (c + 1) * pages_per_block)
        ks = [k_refs[r][...] for r in pages]
        s = [jnp.sum(kp * q[None], axis=-1, keepdims=True) for kp in ks]
        m = s[0].max(axis=0)
        ksum = ks[0].sum(axis=0)
        for sr, kp in zip(s[1:], ks[1:]):
            m = jnp.maximum(m, sr.max(axis=0))
            ksum = ksum + kp.sum(axis=0)
        l = jnp.zeros_like(m)
        o = jnp.zeros(q.shape, F32)
        for sr, r in zip(s, pages):
            p = jnp.exp(sr - m[None])
            l = l + p.sum(axis=0)
            o = o + (p * v_refs[r][...]).sum(axis=0)
        km_ref[c] = ksum * (1.0 / n_keys)
        m_ref[c] = jnp.broadcast_to(m, q.shape)
        l_ref[c] = jnp.broadcast_to(l, q.shape)
        o_ref[c] = o


def _sample_blocks(q, cache_k, cache_v, page_table, pages_per_step=8):
    n_seq, n_pages = page_table.shape
    _, page, n_heads, hd = cache_k.shape
    ppb = MOBA_BLOCK // page
    assert MOBA_BLOCK % page == 0 and n_pages % pages_per_step == 0 and pages_per_step % ppb == 0
    steps = n_pages // pages_per_step
    bps = pages_per_step // ppb

    def page_spec(c):
        return pl.BlockSpec((None, page, n_heads, hd),
                            lambda s, g, pt: (pt[(s * steps + g) * pages_per_step + c], 0, 0, 0))

    blk_out = pl.BlockSpec((None, bps, n_heads, hd), lambda s, g, pt: (s, g, 0, 0))
    grid_spec = pltpu.PrefetchScalarGridSpec(
        num_scalar_prefetch=1,
        grid=(n_seq, steps),
        in_specs=[pl.BlockSpec((None, n_heads, hd), lambda s, g, pt: (s, 0, 0))]
        + [page_spec(c) for c in range(pages_per_step)] * 2,
        out_specs=[blk_out] * 4,
    )
    sds = jax.ShapeDtypeStruct((n_seq, n_pages // ppb, n_heads, hd), F32)
    return pl.pallas_call(
        functools.partial(_sample_blocks_kernel, pages_per_step=pages_per_step, pages_per_block=ppb),
        grid_spec=grid_spec,
        out_shape=[sds] * 4,
        compiler_params=_cparams("arbitrary", "arbitrary"),
        name="sample_blocks",
    )(page_table.reshape(-1), q, *([cache_k] * pages_per_step), *([cache_v] * pages_per_step))


def _sample_combine_kernel(q_ref, kn_ref, vn_ref, km_ref, m_ref, l_ref, o_ref, out_ref):
    q = q_ref[...]
    qb = q.astype(BF16).astype(F32)
    gate = jnp.sum(km_ref[...].astype(BF16).astype(F32) * qb[None], axis=-1, keepdims=True)
    blk_iota = lax.broadcasted_iota(I32, gate.shape, 0)
    ninf = jnp.float32(-jnp.inf)
    sel = jnp.zeros(gate.shape, jnp.bool_)
    for _ in range(MOBA_TOPK):
        g_max = jnp.max(gate, axis=0, keepdims=True)
        idx = jnp.min(jnp.where(gate == g_max, blk_iota, BIG_IDX), axis=0, keepdims=True)
        hit = blk_iota == idx
        sel = sel | (hit & (g_max > ninf))
        gate = jnp.where(hit, ninf, gate)
    s_new = jnp.sum(q * QK_SCALE * kn_ref[...], axis=-1, keepdims=True)
    mb = m_ref[...]
    m_tot = jnp.maximum(jnp.max(jnp.where(sel, mb, ninf), axis=0), s_new)
    w = jnp.where(sel, jnp.exp(mb - m_tot[None]), 0.0)
    p_new = jnp.exp(s_new - m_tot)
    l = jnp.sum(w * l_ref[...], axis=0) + p_new
    o = jnp.sum(w * o_ref[...], axis=0) + p_new * vn_ref[...]
    out_ref[...] = o / l


def _sample_combine(q, k_new, v_new, km, mb, lb, ob):
    n_seq, nblk, n_heads, hd = km.shape
    row = pl.BlockSpec((None, n_heads, hd), lambda s: (s, 0, 0))
    blk = pl.BlockSpec((None, nblk, n_heads, hd), lambda s: (s, 0, 0, 0))
    return pl.pallas_call(
        _sample_combine_kernel,
        grid=(n_seq,),
        in_specs=[row, row, row, blk, blk, blk, blk],
        out_specs=row,
        out_shape=jax.ShapeDtypeStruct((n_seq, n_heads, hd), F32),
        compiler_params=_cparams("arbitrary"),
        name="sample_combine",
    )(q, k_new, v_new, km, mb, lb, ob)


def kernel(x_prompt, x_sample, state_conv, cache_k, cache_v, page_table, norm_mix, norm_ffn, norm_kv,
           w_conv_in, w_conv, w_conv_out, w_kv, k_norm, w_q, q_norm, w_o, w_router_group,
           b_router_group, w_router_expert, b_router_expert, w_expert_gate, w_expert_up, w_expert_down):
    bp, tp, d = x_prompt.shape
    bs = x_sample.shape[0]
    n_heads = d // HEAD_DIM
    kn = k_norm[None, :]
    qn = q_norm[0][None, :]

    def moe(x, layer, tm, tm_rows, blk):
        return _moe(x, layer, norm_ffn[layer][None, :], w_router_group[layer], b_router_group[layer],
                    w_router_expert[layer], b_router_expert[layer],
                    w_expert_gate, w_expert_up, w_expert_down, tm=tm, tm_rows=tm_rows, blk=blk)

    x1, conv_p = _mixer_prompt(x_prompt, norm_mix[0][None, :], w_conv_in[0], w_conv[0], w_conv_out[0])
    x2 = moe(x1.reshape(bp * tp, d), 0, 512, 256, 256)
    k_p, v_p, kb, vt, qt, bias = _kvq_prompt(x2.reshape(bp, tp, d), norm_kv[None, :], norm_mix[1][None, :],
                                             w_kv, w_q[0], kn, qn)
    attn = _moba_prompt(qt, bias, kb, vt)
    x3 = _proj_res(x2, attn.reshape(bp * tp, d), w_o[0], 512)
    y_p = moe(x3, 1, 512, 256, 256).reshape(bp, tp, d)

    xs = x_sample.reshape(bs, d)
    st = state_conv[0]
    x1s, up_s = _mixer_sample(xs, st[:, 0, :], st[:, 1, :], norm_mix[0][None, :],
                              w_conv_in[0], w_conv[0], w_conv_out[0])
    conv_s = jnp.stack([st[:, 1, :], up_s], axis=1)[None]
    x2s = moe(x1s, 0, bs, bs, 16)
    k_s, v_s, q_s = _kvq_sample(x2s, norm_kv[None, :], norm_mix[1][None, :], w_kv, w_q[0], kn, qn)
    heads = lambda z: z.reshape(bs, n_heads, HEAD_DIM)
    km, mb, lb, ob = _sample_blocks(heads(q_s), cache_k, cache_v, page_table)
    attn_s = _sample_combine(heads(q_s), heads(k_s), heads(v_s), km, mb, lb, ob)
    x3s = _proj_res(x2s, attn_s.reshape(bs, d), w_o[0], bs)
    y_s = moe(x3s, 1, bs, bs, 16).reshape(bs, 1, d)

    return (y_p, y_s, conv_p[None],
            k_p.reshape(bp, tp, n_heads, HEAD_DIM), v_p.reshape(bp, tp, n_heads, HEAD_DIM),
            conv_s,
            k_s.reshape(bs, 1, n_heads, HEAD_DIM), v_s.reshape(bs, 1, n_heads, HEAD_DIM))
```

```python
import functools

import jax
import jax.numpy as jnp
from jax import lax
from jax.experimental import pallas as pl
from jax.experimental.pallas import tpu as pltpu

F32 = jnp.float32
BF16 = jnp.bfloat16
I32 = jnp.int32

EPS = 1e-6
HEAD_DIM = 128
MOBA_BLOCK = 256
MOBA_TOPK = 3
N_GROUPS = 4
EXPERTS_PER_GROUP = 8
EXPERT_TOPK = 2
CONV_W = 3

LANE = 128
SUBLANE = 8
VMEM_LIMIT = 48 * 1024 * 1024
NEG = -1e30
BIG_IDX = 1 << 20
LOG2_E = 1.4426950408889634
QK_SCALE = HEAD_DIM ** -0.5
QK_SCALE_LOG2 = QK_SCALE * LOG2_E
V_ONES_ROWS = 16


def _cparams(*sem):
    return pltpu.CompilerParams(dimension_semantics=sem, vmem_limit_bytes=VMEM_LIMIT)


def _rms(x, g):
    ms = jnp.mean(x * x, axis=-1, keepdims=True)
    return x * lax.rsqrt(ms + EPS) * g


def _dot(a, b):
    return jnp.dot(a.astype(BF16), b.astype(BF16), preferred_element_type=F32)


def _mixer_prompt_kernel(x_ref, g_ref, win_ref, wc_ref, wout_ref, o_ref, st_ref, buf_ref, *, tm, d):
    @pl.when(pl.program_id(1) == 0)
    def _():
        buf_ref[0:SUBLANE, :] = jnp.zeros((SUBLANE, d), F32)

    x = x_ref[...]
    h = _rms(x, g_ref[...])
    u = _dot(h, win_ref[...])
    bg = u[:, :d]
    up = u[:, d:2 * d] * u[:, 2 * d:]
    buf_ref[SUBLANE:SUBLANE + tm, :] = up
    up1 = buf_ref[SUBLANE - 1:SUBLANE - 1 + tm, :]
    up2 = buf_ref[SUBLANE - 2:SUBLANE - 2 + tm, :]
    wc = wc_ref[...]
    y = wc[0:1] * up2 + wc[1:2] * up1 + wc[2:3] * up
    o_ref[...] = x + _dot(bg * y, wout_ref[...])
    last2 = buf_ref[tm + SUBLANE - 2:tm + SUBLANE, :]
    st_ref[...] = last2
    buf_ref[SUBLANE - 2:SUBLANE, :] = last2


def _mixer_prompt(x, g, w_in, w_c, w_out, tm=256):
    b, t, d = x.shape
    kern = functools.partial(_mixer_prompt_kernel, tm=tm, d=d)
    return pl.pallas_call(
        kern,
        grid=(b, t // tm),
        in_specs=[
            pl.BlockSpec((None, tm, d), lambda i, j: (i, j, 0)),
            pl.BlockSpec((1, d), lambda i, j: (0, 0)),
            pl.BlockSpec((d, 3 * d), lambda i, j: (0, 0)),
            pl.BlockSpec((CONV_W, d), lambda i, j: (0, 0)),
            pl.BlockSpec((d, d), lambda i, j: (0, 0)),
        ],
        out_specs=[
            pl.BlockSpec((None, tm, d), lambda i, j: (i, j, 0)),
            pl.BlockSpec((None, CONV_W - 1, d), lambda i, j: (i, 0, 0)),
        ],
        out_shape=[
            jax.ShapeDtypeStruct((b, t, d), F32),
            jax.ShapeDtypeStruct((b, CONV_W - 1, d), F32),
        ],
        scratch_shapes=[pltpu.VMEM((tm + SUBLANE, d), F32)],
        compiler_params=_cparams("arbitrary", "arbitrary"),
        name="mixer_prompt",
    )(x, g, w_in.astype(BF16), w_c, w_out.astype(BF16))


def _mixer_sample_kernel(x_ref, p0_ref, p1_ref, g_ref, win_ref, wc_ref, wout_ref, o_ref, up_ref, *, d):
    x = x_ref[...]
    h = _rms(x, g_ref[...])
    u = _dot(h, win_ref[...])
    bg = u[:, :d]
    up = u[:, d:2 * d] * u[:, 2 * d:]
    wc = wc_ref[...]
    y = wc[0:1] * p0_ref[...] + wc[1:2] * p1_ref[...] + wc[2:3] * up
    o_ref[...] = x + _dot(bg * y, wout_ref[...])
    up_ref[...] = up


def _mixer_sample(x, p0, p1, g, w_in, w_c, w_out):
    n, d = x.shape
    return pl.pallas_call(
        functools.partial(_mixer_sample_kernel, d=d),
        out_shape=[jax.ShapeDtypeStruct((n, d), F32), jax.ShapeDtypeStruct((n, d), F32)],
        compiler_params=pltpu.CompilerParams(vmem_limit_bytes=VMEM_LIMIT),
        name="mixer_sample",
    )(x, p0, p1, g, w_in, w_c, w_out)


def _router_kernel(x_ref, g_ref, wr_ref, br_ref, info_ref, cnt_ref, run_ref):
    @pl.when(pl.program_id(0) == 0)
    def _():
        run_ref[...] = jnp.zeros(run_ref.shape, F32)

    xn = _rms(x_ref[...], g_ref[...])
    lg = _dot(xn, wr_ref[...]) + br_ref[...]
    lane = lax.broadcasted_iota(I32, lg.shape, 1)
    ninf = jnp.float32(-jnp.inf)
    is_g = lane < N_GROUPS
    gl = jnp.where(is_g, lg, ninf)
    gmax = jnp.max(gl, axis=-1, keepdims=True)
    gsel = jnp.min(jnp.where(gl == gmax, lane, BIG_IDX), axis=-1, keepdims=True)
    gsum = jnp.sum(jnp.where(is_g, jnp.exp(gl - gmax), 0.0), axis=-1, keepdims=True)
    p_group = 1.0 / gsum
    lo = N_GROUPS + gsel * EXPERTS_PER_GROUP
    in_grp = (lane >= lo) & (lane < lo + EXPERTS_PER_GROUP)
    el = jnp.where(in_grp, lg, ninf)
    e1 = jnp.max(el, axis=-1, keepdims=True)
    i1 = jnp.min(jnp.where(el == e1, lane, BIG_IDX), axis=-1, keepdims=True)
    el2 = jnp.where(lane == i1, ninf, el)
    e2 = jnp.max(el2, axis=-1, keepdims=True)
    i2 = jnp.min(jnp.where(el2 == e2, lane, BIG_IDX), axis=-1, keepdims=True)
    r = jnp.exp(e2 - e1)
    w1 = p_group / (1.0 + r)
    w2 = w1 * r
    eid1 = i1 - N_GROUPS
    eid2 = i2 - N_GROUPS
    hit1 = lane == eid1
    hit2 = lane == eid2
    onehot = jnp.where(hit1 | hit2, 1.0, 0.0)
    tm = onehot.shape[0]
    tri = jnp.where(lax.broadcasted_iota(I32, (tm, tm), 0) > lax.broadcasted_iota(I32, (tm, tm), 1), 1.0, 0.0)
    before = _dot(tri, onehot) + run_ref[0:1, :]
    rank1 = jnp.sum(jnp.where(hit1, before, 0.0), axis=-1, keepdims=True)
    rank2 = jnp.sum(jnp.where(hit2, before, 0.0), axis=-1, keepdims=True)
    total = run_ref[0:1, :] + jnp.sum(onehot, axis=0, keepdims=True)
    run_ref[0:1, :] = total
    cnt_ref[...] = total
    cols = (eid1.astype(F32), eid2.astype(F32), w1, w2, rank1, rank2)
    info = jnp.zeros(lg.shape, F32)
    for c, val in enumerate(cols):
        info = jnp.where(lane == c, val, info)
    info_ref[...] = info


ROUTE_EID, ROUTE_W, ROUTE_RANK = 0, 2, 4


def _router(x, g, wr, br, tm):
    n, d = x.shape
    return pl.pallas_call(
        _router_kernel,
        grid=(n // tm,),
        in_specs=[
            pl.BlockSpec((tm, d), lambda i: (i, 0)),
            pl.BlockSpec((1, d), lambda i: (0, 0)),
            pl.BlockSpec((d, LANE), lambda i: (0, 0)),
            pl.BlockSpec((1, LANE), lambda i: (0, 0)),
        ],
        out_specs=[
            pl.BlockSpec((tm, LANE), lambda i: (i, 0)),
            pl.BlockSpec((1, LANE), lambda i: (0, 0)),
        ],
        out_shape=[jax.ShapeDtypeStruct((n, LANE), F32), jax.ShapeDtypeStruct((1, LANE), F32)],
        scratch_shapes=[pltpu.VMEM((SUBLANE, LANE), F32)],
        compiler_params=_cparams("arbitrary"),
        name="router",
    )(x, g, wr, br)


ISSUE_UNROLL = 8


def _row_copy(src, src_row, dst, dst_row, sem):
    return pltpu.make_async_copy(src.at[pl.ds(src_row, 1), :], dst.at[pl.ds(dst_row, 1), :], sem)


def _dispatch_kernel(dest_ref, x_ref, g_ref, zeros_ref, xs_ref, buf, sem, *, tm, n_steps):
    del zeros_ref
    i = pl.program_id(0)
    slot = i % 2

    def wait_rows(s):
        for _ in range(EXPERT_TOPK):
            pltpu.make_async_copy(buf.at[s], xs_ref.at[pl.ds(0, tm), :], sem.at[s]).wait()

    @pl.when(i >= 2)
    def _():
        wait_rows(slot)

    buf[slot] = _rms(x_ref[...], g_ref[...])

    def issue(r, carry):
        a = (i * tm + r) * EXPERT_TOPK
        for k in range(EXPERT_TOPK):
            _row_copy(buf.at[slot], r, xs_ref, dest_ref[a + k], sem.at[slot]).start()
        return carry

    lax.fori_loop(0, tm, issue, 0, unroll=ISSUE_UNROLL)

    @pl.when(i == n_steps - 1)
    def _():
        if n_steps > 1:
            wait_rows(1 - slot)
        wait_rows(slot)


def _dispatch(x, g, dest, p, tm):
    n, d = x.shape
    n_steps = n // tm
    grid_spec = pltpu.PrefetchScalarGridSpec(
        num_scalar_prefetch=1,
        grid=(n_steps,),
        in_specs=[
            pl.BlockSpec((tm, d), lambda i, dst: (i, 0)),
            pl.BlockSpec((1, d), lambda i, dst: (0, 0)),
            pl.BlockSpec(memory_space=pl.ANY),
        ],
        out_specs=pl.BlockSpec(memory_space=pl.ANY),
        scratch_shapes=[pltpu.VMEM((2, tm, d), F32), pltpu.SemaphoreType.DMA((2,))],
    )
    return pl.pallas_call(
        functools.partial(_dispatch_kernel, tm=tm, n_steps=n_steps),
        grid_spec=grid_spec,
        out_shape=jax.ShapeDtypeStruct((p, d), F32),
        input_output_aliases={3: 0},
        compiler_params=_cparams("arbitrary"),
        name="dispatch",
    )(dest, x, g, jnp.zeros((p, d), F32))


def _combine_kernel(dest_ref, x_ref, info_ref, ys_ref, o_ref, buf, sem, *, tm, n_steps):
    i = pl.program_id(0)
    slot = i % 2

    def issue_step(step, s):
        def issue(r, carry):
            a = (step * tm + r) * EXPERT_TOPK
            for k in range(EXPERT_TOPK):
                _row_copy(ys_ref, dest_ref[a + k], buf.at[s, k], r, sem.at[s]).start()
            return carry
        lax.fori_loop(0, tm, issue, 0, unroll=ISSUE_UNROLL)

    @pl.when(i == 0)
    def _():
        issue_step(0, 0)

    @pl.when(i + 1 < n_steps)
    def _():
        issue_step(i + 1, 1 - slot)

    for k in range(EXPERT_TOPK):
        pltpu.make_async_copy(ys_ref.at[pl.ds(0, tm), :], buf.at[slot, k], sem.at[slot]).wait()
    out = x_ref[...]
    for k in range(EXPERT_TOPK):
        out = out + info_ref[:, ROUTE_W + k:ROUTE_W + k + 1] * buf[slot, k]
    o_ref[...] = out


def _combine(x, info, ys, dest, tm):
    n, d = x.shape
    n_steps = n // tm
    grid_spec = pltpu.PrefetchScalarGridSpec(
        num_scalar_prefetch=1,
        grid=(n_steps,),
        in_specs=[
            pl.BlockSpec((tm, d), lambda i, dst: (i, 0)),
            pl.BlockSpec((tm, LANE), lambda i, dst: (i, 0)),
            pl.BlockSpec(memory_space=pl.ANY),
        ],
        out_specs=pl.BlockSpec((tm, d), lambda i, dst: (i, 0)),
        scratch_shapes=[pltpu.VMEM((2, EXPERT_TOPK, tm, d), F32), pltpu.SemaphoreType.DMA((2,))],
    )
    return pl.pallas_call(
        functools.partial(_combine_kernel, tm=tm, n_steps=n_steps),
        grid_spec=grid_spec,
        out_shape=jax.ShapeDtypeStruct((n, d), F32),
        compiler_params=_cparams("arbitrary"),
        name="combine",
    )(dest, x, info, ys)


def _experts_kernel(be_ref, nu_ref, xs_ref, wg_ref, wu_ref, wd_ref, o_ref, wgb, wub, wdb):
    i = pl.program_id(0)
    e = be_ref[i]
    prev = be_ref[jnp.maximum(i - 1, 0)]

    @pl.when((i == 0) | (e != prev))
    def _():
        wgb[...] = wg_ref[...].astype(BF16)
        wub[...] = wu_ref[...].astype(BF16)
        wdb[...] = wd_ref[...].astype(BF16)

    @pl.when(i < nu_ref[0])
    def _():
        xb = xs_ref[...].astype(BF16)
        h1 = jnp.dot(xb, wgb[...], preferred_element_type=F32)
        h2 = jnp.dot(xb, wub[...], preferred_element_type=F32)
        h = h1 * jax.nn.sigmoid(h1) * h2
        o_ref[...] = jnp.dot(h.astype(BF16), wdb[...], preferred_element_type=F32)

    @pl.when(i >= nu_ref[0])
    def _():
        o_ref[...] = jnp.zeros(o_ref.shape, F32)


def _experts(xs, blk_expert, n_used, wg, wu, wd, layer, blk):
    p, d = xs.shape
    de = wg.shape[-1]
    n_blk = p // blk
    grid_spec = pltpu.PrefetchScalarGridSpec(
        num_scalar_prefetch=2,
        grid=(n_blk,),
        in_specs=[
            pl.BlockSpec((blk, d), lambda i, be, nu: (jnp.minimum(i, nu[0] - 1), 0)),
            pl.BlockSpec((None, None, d, de), lambda i, be, nu: (layer, be[i], 0, 0)),
            pl.BlockSpec((None, None, d, de), lambda i, be, nu: (layer, be[i], 0, 0)),
            pl.BlockSpec((None, None, de, d), lambda i, be, nu: (layer, be[i], 0, 0)),
        ],
        out_specs=pl.BlockSpec((blk, d), lambda i, be, nu: (i, 0)),
        scratch_shapes=[pltpu.VMEM((d, de), BF16), pltpu.VMEM((d, de), BF16), pltpu.VMEM((de, d), BF16)],
    )
    return pl.pallas_call(
        _experts_kernel,
        grid_spec=grid_spec,
        out_shape=jax.ShapeDtypeStruct((p, d), F32),
        compiler_params=_cparams("arbitrary"),
        name="experts",
    )(blk_expert, n_used, xs, wg, wu, wd)


def _moe(x, layer, g, w_rg, b_rg, w_re, b_re, wg, wu, wd, *, tm, tm_rows, blk):
    n, d = x.shape
    n_exp = wg.shape[1]
    pad = LANE - N_GROUPS - n_exp
    wr = jnp.concatenate([w_rg, w_re, jnp.zeros((d, pad), F32)], axis=1)
    br = jnp.concatenate([b_rg, b_re, jnp.zeros((pad,), F32)])[None, :]
    info, cnt = _router(x, g, wr, br, tm)

    a = n * EXPERT_TOPK
    experts = jnp.arange(n_exp, dtype=I32)
    counts = cnt[0, :n_exp].astype(I32)
    padded = ((counts + blk - 1) // blk) * blk
    pad_end = jnp.cumsum(padded)
    pad_start = pad_end - padded
    eid = info[:, ROUTE_EID:ROUTE_EID + EXPERT_TOPK].astype(I32)
    rank = info[:, ROUTE_RANK:ROUTE_RANK + EXPERT_TOPK].astype(I32)
    start = jnp.sum(jnp.where(eid[..., None] == experts, pad_start, 0), axis=-1)
    dest = (start + rank).reshape(a)
    n_blk = -(-a // blk) + n_exp
    n_used = (pad_end[-1] // blk).astype(I32)
    blk_ids = jnp.minimum(jnp.arange(n_blk, dtype=I32), n_used - 1)
    blk_expert = jnp.sum((blk_ids[:, None] * blk >= pad_end[None, :]).astype(I32), axis=1)
    blk_expert = jnp.minimum(blk_expert, n_exp - 1)
    xs = _dispatch(x, g, dest, n_blk * blk, tm_rows)
    ys = _experts(xs, blk_expert, n_used.reshape(1), wg, wu, wd, layer, blk)
    return _combine(x, info, ys, dest, tm_rows)


def _head_rms(x, g):
    return x * lax.rsqrt(jnp.mean(x * x, axis=-1, keepdims=True) + EPS) * g


def _kvq_prompt_kernel(x_ref, gkv_ref, gq_ref, wkv_ref, wq_ref, kn_ref, qn_ref,
                       k_ref, v_ref, kb_ref, vt_ref, qt_ref, bias_ref, km_ref, *, n_heads, nb):
    i = pl.program_id(1)
    d = n_heads * HEAD_DIM

    @pl.when(i == 0)
    def _():
        km_ref[...] = jnp.zeros(km_ref.shape, F32)

    x = x_ref[...]
    kv = _dot(_rms(x, gkv_ref[...]), wkv_ref[...])
    q = _dot(_rms(x, gq_ref[...]), wq_ref[...])
    v_ref[...] = kv[:, d:]
    tq = x.shape[0]
    blk_iota = lax.broadcasted_iota(I32, (nb, tq), 0)
    row_iota = lax.broadcasted_iota(I32, (nb, HEAD_DIM), 0)
    ninf = jnp.float32(-jnp.inf)
    for h in range(n_heads):
        hs = slice(h * HEAD_DIM, (h + 1) * HEAD_DIM)
        kh = _head_rms(kv[:, hs], kn_ref[...])
        k_ref[:, hs] = kh
        kb_ref[h] = kh.astype(BF16)
        vt = kv[:, d + h * HEAD_DIM:d + (h + 1) * HEAD_DIM].T.astype(BF16)
        vt_ref[h] = jnp.concatenate([vt, jnp.ones((V_ONES_ROWS, tq), BF16)], axis=0)
        qt = _head_rms(q[:, hs], qn_ref[...]).T
        qt_ref[h] = (qt * QK_SCALE_LOG2).astype(BF16)
        kmh = km_ref[:, hs]
        gate = _dot(kmh, qt)
        gate = jnp.where(blk_iota < i, gate, ninf)
        sel = jnp.zeros((nb, tq), jnp.bool_)
        for _ in range(MOBA_TOPK):
            m = jnp.max(gate, axis=0, keepdims=True)
            idx = jnp.min(jnp.where(gate == m, blk_iota, BIG_IDX), axis=0, keepdims=True)
            hit = blk_iota == idx
            sel = sel | (hit & (m > ninf))
            gate = jnp.where(hit, ninf, gate)
        bias_ref[h] = jnp.where(sel, 0.0, NEG)
        kmean = jnp.mean(kh, axis=0, keepdims=True)
        km_ref[:, hs] = jnp.where(row_iota == i, kmean, kmh)


def _kvq_prompt(x, gkv, gq, w_kv, w_q, kn, qn):
    b, t, d = x.shape
    n_heads = d // HEAD_DIM
    tq = MOBA_BLOCK
    nb = t // tq
    kern = functools.partial(_kvq_prompt_kernel, n_heads=n_heads, nb=nb)
    const = lambda i, j: (0, 0)
    return pl.pallas_call(
        kern,
        grid=(b, nb),
        in_specs=[
            pl.BlockSpec((None, tq, d), lambda i, j: (i, j, 0)),
            pl.BlockSpec((1, d), const),
            pl.BlockSpec((1, d), const),
            pl.BlockSpec((d, 2 * d), const),
            pl.BlockSpec((d, d), const),
            pl.BlockSpec((1, HEAD_DIM), const),
            pl.BlockSpec((1, HEAD_DIM), const),
        ],
        out_specs=[
            pl.BlockSpec((None, tq, d), lambda i, j: (i, j, 0)),
            pl.BlockSpec((None, tq, d), lambda i, j: (i, j, 0)),
            pl.BlockSpec((None, n_heads, tq, HEAD_DIM), lambda i, j: (i, 0, j, 0)),
            pl.BlockSpec((None, n_heads, None, HEAD_DIM + V_ONES_ROWS, tq), lambda i, j: (i, 0, j, 0, 0)),
            pl.BlockSpec((None, n_heads, None, HEAD_DIM, tq), lambda i, j: (i, 0, j, 0, 0)),
            pl.BlockSpec((None, n_heads, None, nb, tq), lambda i, j: (i, 0, j, 0, 0)),
        ],
        out_shape=[
            jax.ShapeDtypeStruct((b, t, d), F32),
            jax.ShapeDtypeStruct((b, t, d), F32),
            jax.ShapeDtypeStruct((b, n_heads, t, HEAD_DIM), BF16),
            jax.ShapeDtypeStruct((b, n_heads, nb, HEAD_DIM + V_ONES_ROWS, tq), BF16),
            jax.ShapeDtypeStruct((b, n_heads, nb, HEAD_DIM, tq), BF16),
            jax.ShapeDtypeStruct((b, n_heads, nb, nb, tq), F32),
        ],
        scratch_shapes=[pltpu.VMEM((nb, d), F32)],
        compiler_params=_cparams("arbitrary", "arbitrary"),
        name="kvq_prompt",
    )(x, gkv, gq, w_kv.astype(BF16), w_q.astype(BF16), kn, qn)


def _moba_prompt_kernel(qt_ref, bias_ref, kb_ref, vt_ref, o_ref, *, tq, grp, heads):
    i = pl.program_id(2)
    qts = [qt_ref[h] for h in range(heads)]

    def scores(h, n):
        off = pl.multiple_of(n * tq, tq)
        return jnp.dot(kb_ref[h, pl.ds(off, tq), :], qts[h], preferred_element_type=F32)

    def update(h, s, n, m, acc):
        m_new = jnp.maximum(m, jnp.max(s, axis=0, keepdims=True).astype(F32))
        alpha = jnp.exp2(m - m_new)
        p = jnp.exp2(s - m_new.astype(BF16))
        acc = alpha * acc + jnp.dot(vt_ref[h, n], p, preferred_element_type=F32)
        return m_new, acc

    key_i = lax.broadcasted_iota(I32, (tq, tq), 0)
    qry_i = lax.broadcasted_iota(I32, (tq, tq), 1)
    def run_group(blocks, carries):
        s = [[(scores(h, n).astype(BF16) + bias(h).astype(BF16)) if bias is not None else
              jnp.where(key_i <= qry_i, scores(h, n), NEG).astype(BF16) for n, bias in blocks]
             for h in range(heads)]
        carries = list(carries)
        for c, (n, _) in enumerate(blocks):
            for h in range(heads):
                carries[h] = update(h, s[h][c], n, *carries[h])
        return tuple(carries)

    init = (jnp.full((1, tq), NEG, F32), jnp.zeros((HEAD_DIM + V_ONES_ROWS, tq), F32))
    first = [(i, None)] + [(c, functools.partial(lambda h, c: bias_ref[h, c:c + 1, :], c=c))
                           for c in range(grp - 1)]
    carries = run_group(first, (init,) * heads)

    def body(g, carries):
        base = (grp - 1) + g * grp
        blocks = [(base + c, functools.partial(lambda h, n: bias_ref[h, pl.ds(n, 1), :], n=base + c))
                  for c in range(grp)]
        return run_group(blocks, carries)

    n_groups = (jnp.maximum(i - (grp - 1), 0) + grp - 1) // grp
    carries = lax.fori_loop(0, n_groups, body, tuple(carries))
    for h in range(heads):
        _, acc = carries[h]
        out = acc[:HEAD_DIM] / acc[HEAD_DIM:HEAD_DIM + 1]
        o_ref[:, h * HEAD_DIM:(h + 1) * HEAD_DIM] = out.T.astype(o_ref.dtype)


def _moba_prompt(qt, bias, kb, vt, grp=2, heads=8):
    b, n_heads, nb, _, tq = qt.shape
    t = nb * tq
    assert nb % grp == 0
    assert n_heads % heads == 0
    return pl.pallas_call(
        functools.partial(_moba_prompt_kernel, tq=tq, grp=grp, heads=heads),
        grid=(b, n_heads // heads, nb),
        in_specs=[
            pl.BlockSpec((None, heads, None, HEAD_DIM, tq), lambda bi, h, i: (bi, h, i, 0, 0)),
            pl.BlockSpec((None, heads, None, nb, tq), lambda bi, h, i: (bi, h, i, 0, 0)),
            pl.BlockSpec((None, heads, t, HEAD_DIM), lambda bi, h, i: (bi, h, 0, 0),
                         pipeline_mode=pl.Buffered(1)),
            pl.BlockSpec((None, heads, nb, HEAD_DIM + V_ONES_ROWS, tq), lambda bi, h, i: (bi, h, 0, 0, 0),
                         pipeline_mode=pl.Buffered(1)),
        ],
        out_specs=pl.BlockSpec((None, tq, heads * HEAD_DIM), lambda bi, h, i: (bi, i, h)),
        out_shape=jax.ShapeDtypeStruct((b, t, n_heads * HEAD_DIM), BF16),
        compiler_params=_cparams("arbitrary", "arbitrary", "arbitrary"),
        name="moba_prompt",
    )(qt, bias, kb, vt)


def _proj_res_kernel(x_ref, a_ref, w_ref, o_ref):
    o_ref[...] = x_ref[...] + _dot(a_ref[...], w_ref[...])


def _proj_res(x, a, w, tm):
    n, d = x.shape
    w = w.astype(BF16)
    return pl.pallas_call(
        _proj_res_kernel,
        grid=(n // tm,),
        in_specs=[
            pl.BlockSpec((tm, d), lambda i: (i, 0)),
            pl.BlockSpec((tm, a.shape[1]), lambda i: (i, 0)),
            pl.BlockSpec(w.shape, lambda i: (0, 0)),
        ],
        out_specs=pl.BlockSpec((tm, d), lambda i: (i, 0)),
        out_shape=jax.ShapeDtypeStruct((n, d), F32),
        compiler_params=_cparams("arbitrary"),
        name="proj_res",
    )(x, a, w)


def _kvq_sample_kernel(x_ref, gkv_ref, gq_ref, wkv_ref, wq_ref, kn_ref, qn_ref, k_ref, v_ref, q_ref,
                       *, n_heads):
    d = n_heads * HEAD_DIM
    x = x_ref[...]
    kv = _dot(_rms(x, gkv_ref[...]), wkv_ref[...])
    q = _dot(_rms(x, gq_ref[...]), wq_ref[...])
    v_ref[...] = kv[:, d:]
    for h in range(n_heads):
        hs = slice(h * HEAD_DIM, (h + 1) * HEAD_DIM)
        k_ref[:, hs] = _head_rms(kv[:, hs], kn_ref[...])
        q_ref[:, hs] = _head_rms(q[:, hs], qn_ref[...])


def _kvq_sample(x, gkv, gq, w_kv, w_q, kn, qn):
    n, d = x.shape
    sds = jax.ShapeDtypeStruct((n, d), F32)
    return pl.pallas_call(
        functools.partial(_kvq_sample_kernel, n_heads=d // HEAD_DIM),
        out_shape=[sds, sds, sds],
        compiler_params=pltpu.CompilerParams(vmem_limit_bytes=VMEM_LIMIT),
        name="kvq_sample",
    )(x, gkv, gq, w_kv, w_q, kn, qn)


def _sample_blocks_kernel(pt_ref, q_ref, *refs, pages_per_step, pages_per_block):
    del pt_ref
    k_refs = refs[:pages_per_step]
    v_refs = refs[pages_per_step:2 * pages_per_step]
    km_ref, m_ref, l_ref, o_ref = refs[2 * pages_per_step:]
    q = q_ref[...] * QK_SCALE
    n_keys = pages_per_block * k_refs[0].shape[0]
    for c in range(pages_per_step // pages_per_block):
        pages = range(c * pages_per_block, (c + 1) * pages_per_block)
        ks = [k_refs[r][...] for r in pages]
        s = [jnp.sum(kp * q[None], axis=-1, keepdims=True) for kp in ks]
        m = s[0].max(axis=0)
        ksum = ks[0].sum(axis=0)
        for sr, kp in zip(s[1:], ks[1:]):
            m = jnp.maximum(m, sr.max(axis=0))
            ksum = ksum + kp.sum(axis=0)
        l = jnp.zeros_like(m)
        o = jnp.zeros(q.shape, F32)
        for sr, r in zip(s, pages):
            p = jnp.exp(sr - m[None])
            l = l + p.sum(axis=0)
            o = o + (p * v_refs[r][...]).sum(axis=0)
        km_ref[c] = ksum * (1.0 / n_keys)
        m_ref[c] = jnp.broadcast_to(m, q.shape)
        l_ref[c] = jnp.broadcast_to(l, q.shape)
        o_ref[c] = o


def _sample_blocks(q, cache_k, cache_v, page_table, pages_per_step=16):
    n_seq, n_pages = page_table.shape
    _, page, n_heads, hd = cache_k.shape
    ppb = MOBA_BLOCK // page
    assert MOBA_BLOCK % page == 0 and n_pages % pages_per_step == 0 and pages_per_step % ppb == 0
    steps = n_pages // pages_per_step
    bps = pages_per_step // ppb

    def page_spec(c):
        return pl.BlockSpec((None, page, n_heads, hd),
                            lambda s, g, pt: (pt[(s * steps + g) * pages_per_step + c], 0, 0, 0))

    blk_out = pl.BlockSpec((None, bps, n_heads, hd), lambda s, g, pt: (s, g, 0, 0))
    grid_spec = pltpu.PrefetchScalarGridSpec(
        num_scalar_prefetch=1,
        grid=(n_seq, steps),
        in_specs=[pl.BlockSpec((None, n_heads, hd), lambda s, g, pt: (s, 0, 0))]
        + [page_spec(c) for c in range(pages_per_step)] * 2,
        out_specs=[blk_out] * 4,
    )
    sds = jax.ShapeDtypeStruct((n_seq, n_pages // ppb, n_heads, hd), F32)
    return pl.pallas_call(
        functools.partial(_sample_blocks_kernel, pages_per_step=pages_per_step, pages_per_block=ppb),
        grid_spec=grid_spec,
        out_shape=[sds] * 4,
        compiler_params=_cparams("arbitrary", "arbitrary"),
        name="sample_blocks",
    )(page_table.reshape(-1), q, *([cache_k] * pages_per_step), *([cache_v] * pages_per_step))


def _sample_combine_kernel(q_ref, kn_ref, vn_ref, km_ref, m_ref, l_ref, o_ref, out_ref):
    q = q_ref[...]
    qb = q.astype(BF16).astype(F32)
    gate = jnp.sum(km_ref[...].astype(BF16).astype(F32) * qb[None], axis=-1, keepdims=True)
    blk_iota = lax.broadcasted_iota(I32, gate.shape, 0)
    ninf = jnp.float32(-jnp.inf)
    sel = jnp.zeros(gate.shape, jnp.bool_)
    for _ in range(MOBA_TOPK):
        g_max = jnp.max(gate, axis=0, keepdims=True)
        idx = jnp.min(jnp.where(gate == g_max, blk_iota, BIG_IDX), axis=0, keepdims=True)
        hit = blk_iota == idx
        sel = sel | (hit & (g_max > ninf))
        gate = jnp.where(hit, ninf, gate)
    s_new = jnp.sum(q * QK_SCALE * kn_ref[...], axis=-1, keepdims=True)
    mb = m_ref[...]
    m_tot = jnp.maximum(jnp.max(jnp.where(sel, mb, ninf), axis=0), s_new)
    w = jnp.where(sel, jnp.exp(mb - m_tot[None]), 0.0)
    p_new = jnp.exp(s_new - m_tot)
    l = jnp.sum(w * l_ref[...], axis=0) + p_new
    o = jnp.sum(w * o_ref[...], axis=0) + p_new * vn_ref[...]
    out_ref[...] = o / l


def _sample_combine(q, k_new, v_new, km, mb, lb, ob):
    n_seq, nblk, n_heads, hd = km.shape
    row = pl.BlockSpec((None, n_heads, hd), lambda s: (s, 0, 0))
    blk = pl.BlockSpec((None, nblk, n_heads, hd), lambda s: (s, 0, 0, 0))
    return pl.pallas_call(
        _sample_combine_kernel,
        grid=(n_seq,),
        in_specs=[row, row, row, blk, blk, blk, blk],
        out_specs=row,
        out_shape=jax.ShapeDtypeStruct((n_seq, n_heads, hd), F32),
        compiler_params=_cparams("arbitrary"),
        name="sample_combine",
    )(q, k_new, v_new, km, mb, lb, ob)


def kernel(x_prompt, x_sample, state_conv, cache_k, cache_v, page_table, norm_mix, norm_ffn, norm_kv,
           w_conv_in, w_conv, w_conv_out, w_kv, k_norm, w_q, q_norm, w_o, w_router_group,
           b_router_group, w_router_expert, b_router_expert, w_expert_gate, w_expert_up, w_expert_down):
    bp, tp, d = x_prompt.shape
    bs = x_sample.shape[0]
    n_heads = d // HEAD_DIM
    kn = k_norm[None, :]
    qn = q_norm[0][None, :]

    def moe(x, layer, tm, tm_rows, blk):
        return _moe(x, layer, norm_ffn[layer][None, :], w_router_group[layer], b_router_group[layer],
                    w_router_expert[layer], b_router_expert[layer],
                    w_expert_gate, w_expert_up, w_expert_down, tm=tm, tm_rows=tm_rows, blk=blk)

    x1, conv_p = _mixer_prompt(x_prompt, norm_mix[0][None, :], w_conv_in[0], w_conv[0], w_conv_out[0])
    x2 = moe(x1.reshape(bp * tp, d), 0, 512, 256, 256)
    k_p, v_p, kb, vt, qt, bias = _kvq_prompt(x2.reshape(bp, tp, d), norm_kv[None, :], norm_mix[1][None, :],
                                             w_kv, w_q[0], kn, qn)
    attn = _moba_prompt(qt, bias, kb, vt)
    x3 = _proj_res(x2, attn.reshape(bp * tp, d), w_o[0], 512)
    y_p = moe(x3, 1, 512, 256, 256).reshape(bp, tp, d)

    xs = x_sample.reshape(bs, d)
    st = state_conv[0]
    x1s, up_s = _mixer_sample(xs, st[:, 0, :], st[:, 1, :], norm_mix[0][None, :],
                              w_conv_in[0], w_conv[0], w_conv_out[0])
    conv_s = jnp.stack([st[:, 1, :], up_s], axis=1)[None]
    x2s = moe(x1s, 0, bs, bs, 16)
    k_s, v_s, q_s = _kvq_sample(x2s, norm_kv[None, :], norm_mix[1][None, :], w_kv, w_q[0], kn, qn)
    heads = lambda z: z.reshape(bs, n_heads, HEAD_DIM)
    km, mb, lb, ob = _sample_blocks(heads(q_s), cache_k, cache_v, page_table)
    attn_s = _sample_combine(heads(q_s), heads(k_s), heads(v_s), km, mb, lb, ob)
    x3s = _proj_res(x2s, attn_s.reshape(bs, d), w_o[0], bs)
    y_s = moe(x3s, 1, bs, bs, 16).reshape(bs, 1, d)

    return (y_p, y_s, conv_p[None],
            k_p.reshape(bp, tp, n_heads, HEAD_DIM), v_p.reshape(bp, tp, n_heads, HEAD_DIM),
            conv_s,
            k_s.reshape(bs, 1, n_heads, HEAD_DIM), v_s.reshape(bs, 1, n_heads, HEAD_DIM))
```

```python
import functools

import jax
import jax.numpy as jnp
from jax import lax
from jax.experimental import pallas as pl
from jax.experimental.pallas import tpu as pltpu

F32 = jnp.float32
BF16 = jnp.bfloat16
I32 = jnp.int32

EPS = 1e-6
HEAD_DIM = 128
MOBA_BLOCK = 256
MOBA_TOPK = 3
N_GROUPS = 4
EXPERTS_PER_GROUP = 8
EXPERT_TOPK = 2
CONV_W = 3

LANE = 128
SUBLANE = 8
VMEM_LIMIT = 48 * 1024 * 1024
NEG = -1e30
BIG_IDX = 1 << 20
LOG2_E = 1.4426950408889634
QK_SCALE = HEAD_DIM ** -0.5
QK_SCALE_LOG2 = QK_SCALE * LOG2_E
V_ONES_ROWS = 16


def _cparams(*sem):
    return pltpu.CompilerParams(dimension_semantics=sem, vmem_limit_bytes=VMEM_LIMIT)


def _rms(x, g):
    ms = jnp.mean(x * x, axis=-1, keepdims=True)
    return x * lax.rsqrt(ms + EPS) * g


def _dot(a, b):
    return jnp.dot(a.astype(BF16), b.astype(BF16), preferred_element_type=F32)


def _mixer_prompt_kernel(x_ref, g_ref, win_ref, wc_ref, wout_ref, o_ref, st_ref, buf_ref, *, tm, d):
    @pl.when(pl.program_id(1) == 0)
    def _():
        buf_ref[0:SUBLANE, :] = jnp.zeros((SUBLANE, d), F32)

    x = x_ref[...]
    h = _rms(x, g_ref[...])
    u = _dot(h, win_ref[...])
    bg = u[:, :d]
    up = u[:, d:2 * d] * u[:, 2 * d:]
    buf_ref[SUBLANE:SUBLANE + tm, :] = up
    up1 = buf_ref[SUBLANE - 1:SUBLANE - 1 + tm, :]
    up2 = buf_ref[SUBLANE - 2:SUBLANE - 2 + tm, :]
    wc = wc_ref[...]
    y = wc[0:1] * up2 + wc[1:2] * up1 + wc[2:3] * up
    o_ref[...] = x + _dot(bg * y, wout_ref[...])
    last2 = buf_ref[tm + SUBLANE - 2:tm + SUBLANE, :]
    st_ref[...] = last2
    buf_ref[SUBLANE - 2:SUBLANE, :] = last2


def _mixer_prompt(x, g, w_in, w_c, w_out, tm=256):
    b, t, d = x.shape
    kern = functools.partial(_mixer_prompt_kernel, tm=tm, d=d)
    return pl.pallas_call(
        kern,
        grid=(b, t // tm),
        in_specs=[
            pl.BlockSpec((None, tm, d), lambda i, j: (i, j, 0)),
            pl.BlockSpec((1, d), lambda i, j: (0, 0)),
            pl.BlockSpec((d, 3 * d), lambda i, j: (0, 0)),
            pl.BlockSpec((CONV_W, d), lambda i, j: (0, 0)),
            pl.BlockSpec((d, d), lambda i, j: (0, 0)),
        ],
        out_specs=[
            pl.BlockSpec((None, tm, d), lambda i, j: (i, j, 0)),
            pl.BlockSpec((None, CONV_W - 1, d), lambda i, j: (i, 0, 0)),
        ],
        out_shape=[
            jax.ShapeDtypeStruct((b, t, d), F32),
            jax.ShapeDtypeStruct((b, CONV_W - 1, d), F32),
        ],
        scratch_shapes=[pltpu.VMEM((tm + SUBLANE, d), F32)],
        compiler_params=_cparams("arbitrary", "arbitrary"),
        name="mixer_prompt",
    )(x, g, w_in.astype(BF16), w_c, w_out.astype(BF16))


def _mixer_sample_kernel(x_ref, p0_ref, p1_ref, g_ref, win_ref, wc_ref, wout_ref, o_ref, up_ref, *, d):
    x = x_ref[...]
    h = _rms(x, g_ref[...])
    u = _dot(h, win_ref[...])
    bg = u[:, :d]
    up = u[:, d:2 * d] * u[:, 2 * d:]
    wc = wc_ref[...]
    y = wc[0:1] * p0_ref[...] + wc[1:2] * p1_ref[...] + wc[2:3] * up
    o_ref[...] = x + _dot(bg * y, wout_ref[...])
    up_ref[...] = up


def _mixer_sample(x, p0, p1, g, w_in, w_c, w_out):
    n, d = x.shape
    return pl.pallas_call(
        functools.partial(_mixer_sample_kernel, d=d),
        out_shape=[jax.ShapeDtypeStruct((n, d), F32), jax.ShapeDtypeStruct((n, d), F32)],
        compiler_params=pltpu.CompilerParams(vmem_limit_bytes=VMEM_LIMIT),
        name="mixer_sample",
    )(x, p0, p1, g, w_in, w_c, w_out)


def _token_tile(main_ref, tail_ref, n_main):
    return jnp.where(pl.program_id(0) < n_main, main_ref[...], tail_ref[...])


def _token_specs(tm, d, n_main):
    return [pl.BlockSpec((tm, d), lambda i, *_: (jnp.minimum(i, n_main - 1), 0)),
            pl.BlockSpec((tm, d), lambda i, *_: (jnp.maximum(i - n_main, 0), 0))]


def _router_kernel(xm_ref, xt_ref, g_ref, wr_ref, br_ref, info_ref, cnt_ref, run_ref, *, n_main):
    @pl.when(pl.program_id(0) == 0)
    def _():
        run_ref[...] = jnp.zeros(run_ref.shape, F32)

    xn = _rms(_token_tile(xm_ref, xt_ref, n_main), g_ref[...])
    lg = _dot(xn, wr_ref[...]) + br_ref[...]
    lane = lax.broadcasted_iota(I32, lg.shape, 1)
    ninf = jnp.float32(-jnp.inf)
    is_g = lane < N_GROUPS
    gl = jnp.where(is_g, lg, ninf)
    gmax = jnp.max(gl, axis=-1, keepdims=True)
    gsel = jnp.min(jnp.where(gl == gmax, lane, BIG_IDX), axis=-1, keepdims=True)
    gsum = jnp.sum(jnp.where(is_g, jnp.exp(gl - gmax), 0.0), axis=-1, keepdims=True)
    p_group = 1.0 / gsum
    lo = N_GROUPS + gsel * EXPERTS_PER_GROUP
    in_grp = (lane >= lo) & (lane < lo + EXPERTS_PER_GROUP)
    el = jnp.where(in_grp, lg, ninf)
    e1 = jnp.max(el, axis=-1, keepdims=True)
    i1 = jnp.min(jnp.where(el == e1, lane, BIG_IDX), axis=-1, keepdims=True)
    el2 = jnp.where(lane == i1, ninf, el)
    e2 = jnp.max(el2, axis=-1, keepdims=True)
    i2 = jnp.min(jnp.where(el2 == e2, lane, BIG_IDX), axis=-1, keepdims=True)
    r = jnp.exp(e2 - e1)
    w1 = p_group / (1.0 + r)
    w2 = w1 * r
    eid1 = i1 - N_GROUPS
    eid2 = i2 - N_GROUPS
    hit1 = lane == eid1
    hit2 = lane == eid2
    onehot = jnp.where(hit1 | hit2, 1.0, 0.0)
    tm = onehot.shape[0]
    tri = jnp.where(lax.broadcasted_iota(I32, (tm, tm), 0) > lax.broadcasted_iota(I32, (tm, tm), 1), 1.0, 0.0)
    before = _dot(tri, onehot) + run_ref[0:1, :]
    rank1 = jnp.sum(jnp.where(hit1, before, 0.0), axis=-1, keepdims=True)
    rank2 = jnp.sum(jnp.where(hit2, before, 0.0), axis=-1, keepdims=True)
    total = run_ref[0:1, :] + jnp.sum(onehot, axis=0, keepdims=True)
    run_ref[0:1, :] = total
    cnt_ref[...] = total
    cols = (eid1.astype(F32), eid2.astype(F32), w1, w2, rank1, rank2)
    info = jnp.zeros(lg.shape, F32)
    for c, val in enumerate(cols):
        info = jnp.where(lane == c, val, info)
    info_ref[...] = info


ROUTE_EID, ROUTE_W, ROUTE_RANK = 0, 2, 4
MOE_TILE = 256
MOE_ROW_BLOCK = 512


def _router(x_main, x_tail, g, wr, br, tm):
    d = x_main.shape[1]
    n_main = x_main.shape[0] // tm
    n = x_main.shape[0] + x_tail.shape[0]
    return pl.pallas_call(
        functools.partial(_router_kernel, n_main=n_main),
        grid=(n // tm,),
        in_specs=_token_specs(tm, d, n_main) + [
            pl.BlockSpec((1, d), lambda i: (0, 0)),
            pl.BlockSpec((d, LANE), lambda i: (0, 0)),
            pl.BlockSpec((1, LANE), lambda i: (0, 0)),
        ],
        out_specs=[
            pl.BlockSpec((tm, LANE), lambda i: (i, 0)),
            pl.BlockSpec((1, LANE), lambda i: (0, 0)),
        ],
        out_shape=[jax.ShapeDtypeStruct((n, LANE), F32), jax.ShapeDtypeStruct((1, LANE), F32)],
        scratch_shapes=[pltpu.VMEM((SUBLANE, LANE), F32)],
        compiler_params=_cparams("arbitrary"),
        name="router",
    )(x_main, x_tail, g, wr, br)


ISSUE_UNROLL = 8


def _row_copy(src, src_row, dst, dst_row, sem):
    return pltpu.make_async_copy(src.at[pl.ds(src_row, 1), :], dst.at[pl.ds(dst_row, 1), :], sem)


def _dispatch_kernel(dest_ref, xm_ref, xt_ref, g_ref, zeros_ref, xs_ref, buf, sem, *, tm, n_steps, n_main):
    del zeros_ref
    i = pl.program_id(0)
    slot = i % 2

    def wait_rows(s):
        for _ in range(EXPERT_TOPK):
            pltpu.make_async_copy(buf.at[s], xs_ref.at[pl.ds(0, tm), :], sem.at[s]).wait()

    @pl.when(i >= 2)
    def _():
        wait_rows(slot)

    buf[slot] = _rms(_token_tile(xm_ref, xt_ref, n_main), g_ref[...])

    def issue(r, carry):
        a = (i * tm + r) * EXPERT_TOPK
        for k in range(EXPERT_TOPK):
            _row_copy(buf.at[slot], r, xs_ref, dest_ref[a + k], sem.at[slot]).start()
        return carry

    lax.fori_loop(0, tm, issue, 0, unroll=ISSUE_UNROLL)

    @pl.when(i == n_steps - 1)
    def _():
        if n_steps > 1:
            wait_rows(1 - slot)
        wait_rows(slot)


def _dispatch(x_main, x_tail, g, dest, p, tm):
    d = x_main.shape[1]
    n_main = x_main.shape[0] // tm
    n_steps = n_main + x_tail.shape[0] // tm
    grid_spec = pltpu.PrefetchScalarGridSpec(
        num_scalar_prefetch=1,
        grid=(n_steps,),
        in_specs=_token_specs(tm, d, n_main) + [
            pl.BlockSpec((1, d), lambda i, dst: (0, 0)),
            pl.BlockSpec(memory_space=pl.ANY),
        ],
        out_specs=pl.BlockSpec(memory_space=pl.ANY),
        scratch_shapes=[pltpu.VMEM((2, tm, d), F32), pltpu.SemaphoreType.DMA((2,))],
    )
    return pl.pallas_call(
        functools.partial(_dispatch_kernel, tm=tm, n_steps=n_steps, n_main=n_main),
        grid_spec=grid_spec,
        out_shape=jax.ShapeDtypeStruct((p, d), F32),
        input_output_aliases={4: 0},
        compiler_params=_cparams("arbitrary"),
        name="dispatch",
    )(dest, x_main, x_tail, g, jnp.zeros((p, d), F32))


def _combine_kernel(dest_ref, xm_ref, xt_ref, info_ref, ys_ref, om_ref, ot_ref, buf, sem, *,
                    tm, n_steps, n_main):
    i = pl.program_id(0)
    slot = i % 2

    def issue_step(step, s):
        def issue(r, carry):
            a = (step * tm + r) * EXPERT_TOPK
            for k in range(EXPERT_TOPK):
                _row_copy(ys_ref, dest_ref[a + k], buf.at[s, k], r, sem.at[s]).start()
            return carry
        lax.fori_loop(0, tm, issue, 0, unroll=ISSUE_UNROLL)

    @pl.when(i == 0)
    def _():
        issue_step(0, 0)

    @pl.when(i + 1 < n_steps)
    def _():
        issue_step(i + 1, 1 - slot)

    for k in range(EXPERT_TOPK):
        pltpu.make_async_copy(ys_ref.at[pl.ds(0, tm), :], buf.at[slot, k], sem.at[slot]).wait()
    out = _token_tile(xm_ref, xt_ref, n_main)
    for k in range(EXPERT_TOPK):
        out = out + info_ref[:, ROUTE_W + k:ROUTE_W + k + 1] * buf[slot, k]

    @pl.when(i < n_main)
    def _():
        om_ref[...] = out

    @pl.when(i >= n_main)
    def _():
        ot_ref[...] = out


def _combine(x_main, x_tail, info, ys, dest, tm):
    d = x_main.shape[1]
    n_main = x_main.shape[0] // tm
    n_steps = n_main + x_tail.shape[0] // tm
    grid_spec = pltpu.PrefetchScalarGridSpec(
        num_scalar_prefetch=1,
        grid=(n_steps,),
        in_specs=_token_specs(tm, d, n_main) + [
            pl.BlockSpec((tm, LANE), lambda i, dst: (i, 0)),
            pl.BlockSpec(memory_space=pl.ANY),
        ],
        out_specs=_token_specs(tm, d, n_main),
        scratch_shapes=[pltpu.VMEM((2, EXPERT_TOPK, tm, d), F32), pltpu.SemaphoreType.DMA((2,))],
    )
    return pl.pallas_call(
        functools.partial(_combine_kernel, tm=tm, n_steps=n_steps, n_main=n_main),
        grid_spec=grid_spec,
        out_shape=[jax.ShapeDtypeStruct(x_main.shape, F32), jax.ShapeDtypeStruct(x_tail.shape, F32)],
        compiler_params=_cparams("arbitrary"),
        name="combine",
    )(dest, x_main, x_tail, info, ys)


def _experts_kernel(be_ref, nu_ref, xs_ref, wg_ref, wu_ref, wd_ref, o_ref, wgb, wub, wdb):
    i = pl.program_id(0)
    e = be_ref[i]
    prev = be_ref[jnp.maximum(i - 1, 0)]

    @pl.when((i == 0) | (e != prev))
    def _():
        wgb[...] = wg_ref[...].astype(BF16)
        wub[...] = wu_ref[...].astype(BF16)
        wdb[...] = wd_ref[...].astype(BF16)

    @pl.when(i < nu_ref[0])
    def _():
        xb = xs_ref[...].astype(BF16)
        h1 = jnp.dot(xb, wgb[...], preferred_element_type=F32)
        h2 = jnp.dot(xb, wub[...], preferred_element_type=F32)
        h = h1 * jax.nn.sigmoid(h1) * h2
        o_ref[...] = jnp.dot(h.astype(BF16), wdb[...], preferred_element_type=F32)

    @pl.when(i >= nu_ref[0])
    def _():
        o_ref[...] = jnp.zeros(o_ref.shape, F32)


def _experts(xs, blk_expert, n_used, wg, wu, wd, layer, blk):
    p, d = xs.shape
    de = wg.shape[-1]
    n_blk = p // blk
    grid_spec = pltpu.PrefetchScalarGridSpec(
        num_scalar_prefetch=2,
        grid=(n_blk,),
        in_specs=[
            pl.BlockSpec((blk, d), lambda i, be, nu: (jnp.minimum(i, nu[0] - 1), 0)),
            pl.BlockSpec((None, None, d, de), lambda i, be, nu: (layer, be[i], 0, 0)),
            pl.BlockSpec((None, None, d, de), lambda i, be, nu: (layer, be[i], 0, 0)),
            pl.BlockSpec((None, None, de, d), lambda i, be, nu: (layer, be[i], 0, 0)),
        ],
        out_specs=pl.BlockSpec((blk, d), lambda i, be, nu: (i, 0)),
        scratch_shapes=[pltpu.VMEM((d, de), BF16), pltpu.VMEM((d, de), BF16), pltpu.VMEM((de, d), BF16)],
    )
    return pl.pallas_call(
        _experts_kernel,
        grid_spec=grid_spec,
        out_shape=jax.ShapeDtypeStruct((p, d), F32),
        compiler_params=_cparams("arbitrary"),
        name="experts",
    )(blk_expert, n_used, xs, wg, wu, wd)


def _moe(x_main, x_tail, layer, g, w_rg, b_rg, w_re, b_re, wg, wu, wd, *, tm, blk):
    d = x_main.shape[1]
    n = x_main.shape[0] + x_tail.shape[0]
    n_exp = wg.shape[1]
    pad = LANE - N_GROUPS - n_exp
    wr = jnp.concatenate([w_rg, w_re, jnp.zeros((d, pad), F32)], axis=1)
    br = jnp.concatenate([b_rg, b_re, jnp.zeros((pad,), F32)])[None, :]
    info, cnt = _router(x_main, x_tail, g, wr, br, tm)

    a = n * EXPERT_TOPK
    experts = jnp.arange(n_exp, dtype=I32)
    counts = cnt[0, :n_exp].astype(I32)
    padded = ((counts + blk - 1) // blk) * blk
    pad_end = jnp.cumsum(padded)
    pad_start = pad_end - padded
    eid = info[:, ROUTE_EID:ROUTE_EID + EXPERT_TOPK].astype(I32)
    rank = info[:, ROUTE_RANK:ROUTE_RANK + EXPERT_TOPK].astype(I32)
    start = jnp.sum(jnp.where(eid[..., None] == experts, pad_start, 0), axis=-1)
    dest = (start + rank).reshape(a)
    n_blk = -(-a // blk) + n_exp
    n_used = (pad_end[-1] // blk).astype(I32)
    blk_ids = jnp.minimum(jnp.arange(n_blk, dtype=I32), n_used - 1)
    blk_expert = jnp.sum((blk_ids[:, None] * blk >= pad_end[None, :]).astype(I32), axis=1)
    blk_expert = jnp.minimum(blk_expert, n_exp - 1)
    xs = _dispatch(x_main, x_tail, g, dest, n_blk * blk, tm)
    ys = _experts(xs, blk_expert, n_used.reshape(1), wg, wu, wd, layer, blk)
    return _combine(x_main, x_tail, info, ys, dest, tm)


def _head_rms(x, g):
    return x * lax.rsqrt(jnp.mean(x * x, axis=-1, keepdims=True) + EPS) * g


def _kvq_prompt_kernel(x_ref, gkv_ref, gq_ref, wkv_ref, wq_ref, kn_ref, qn_ref,
                       k_ref, v_ref, kb_ref, vt_ref, qt_ref, bias_ref, km_ref, *, n_heads, nb):
    i = pl.program_id(1)
    d = n_heads * HEAD_DIM

    @pl.when(i == 0)
    def _():
        km_ref[...] = jnp.zeros(km_ref.shape, F32)

    x = x_ref[...]
    kv = _dot(_rms(x, gkv_ref[...]), wkv_ref[...])
    q = _dot(_rms(x, gq_ref[...]), wq_ref[...])
    v_ref[...] = kv[:, d:]
    tq = x.shape[0]
    blk_iota = lax.broadcasted_iota(I32, (nb, tq), 0)
    row_iota = lax.broadcasted_iota(I32, (nb, HEAD_DIM), 0)
    ninf = jnp.float32(-jnp.inf)
    for h in range(n_heads):
        hs = slice(h * HEAD_DIM, (h + 1) * HEAD_DIM)
        kh = _head_rms(kv[:, hs], kn_ref[...])
        k_ref[:, hs] = kh
        kb_ref[h] = kh.astype(BF16)
        vt = kv[:, d + h * HEAD_DIM:d + (h + 1) * HEAD_DIM].T.astype(BF16)
        vt_ref[h] = jnp.concatenate([vt, jnp.ones((V_ONES_ROWS, tq), BF16)], axis=0)
        qt = _head_rms(q[:, hs], qn_ref[...]).T
        qt_ref[h] = (qt * QK_SCALE_LOG2).astype(BF16)
        kmh = km_ref[:, hs]
        gate = _dot(kmh, qt)
        gate = jnp.where(blk_iota < i, gate, ninf)
        sel = jnp.zeros((nb, tq), jnp.bool_)
        for _ in range(MOBA_TOPK):
            m = jnp.max(gate, axis=0, keepdims=True)
            idx = jnp.min(jnp.where(gate == m, blk_iota, BIG_IDX), axis=0, keepdims=True)
            hit = blk_iota == idx
            sel = sel | (hit & (m > ninf))
            gate = jnp.where(hit, ninf, gate)
        bias_ref[h] = jnp.where(sel, 0.0, NEG)
        kmean = jnp.mean(kh, axis=0, keepdims=True)
        km_ref[:, hs] = jnp.where(row_iota == i, kmean, kmh)


def _kvq_prompt(x, gkv, gq, w_kv, w_q, kn, qn):
    b, t, d = x.shape
    n_heads = d // HEAD_DIM
    tq = MOBA_BLOCK
    nb = t // tq
    kern = functools.partial(_kvq_prompt_kernel, n_heads=n_heads, nb=nb)
    const = lambda i, j: (0, 0)
    return pl.pallas_call(
        kern,
        grid=(b, nb),
        in_specs=[
            pl.BlockSpec((None, tq, d), lambda i, j: (i, j, 0)),
            pl.BlockSpec((1, d), const),
            pl.BlockSpec((1, d), const),
            pl.BlockSpec((d, 2 * d), const),
            pl.BlockSpec((d, d), const),
            pl.BlockSpec((1, HEAD_DIM), const),
            pl.BlockSpec((1, HEAD_DIM), const),
        ],
        out_specs=[
            pl.BlockSpec((None, tq, d), lambda i, j: (i, j, 0)),
            pl.BlockSpec((None, tq, d), lambda i, j: (i, j, 0)),
            pl.BlockSpec((None, n_heads, tq, HEAD_DIM), lambda i, j: (i, 0, j, 0)),
            pl.BlockSpec((None, n_heads, None, HEAD_DIM + V_ONES_ROWS, tq), lambda i, j: (i, 0, j, 0, 0)),
            pl.BlockSpec((None, n_heads, None, HEAD_DIM, tq), lambda i, j: (i, 0, j, 0, 0)),
            pl.BlockSpec((None, n_heads, None, nb, tq), lambda i, j: (i, 0, j, 0, 0)),
        ],
        out_shape=[
            jax.ShapeDtypeStruct((b, t, d), F32),
            jax.ShapeDtypeStruct((b, t, d), F32),
            jax.ShapeDtypeStruct((b, n_heads, t, HEAD_DIM), BF16),
            jax.ShapeDtypeStruct((b, n_heads, nb, HEAD_DIM + V_ONES_ROWS, tq), BF16),
            jax.ShapeDtypeStruct((b, n_heads, nb, HEAD_DIM, tq), BF16),
            jax.ShapeDtypeStruct((b, n_heads, nb, nb, tq), F32),
        ],
        scratch_shapes=[pltpu.VMEM((nb, d), F32)],
        compiler_params=_cparams("arbitrary", "arbitrary"),
        name="kvq_prompt",
    )(x, gkv, gq, w_kv.astype(BF16), w_q.astype(BF16), kn, qn)


def _moba_prompt_kernel(qt_ref, bias_ref, kb_ref, vt_ref, o_ref, *, tq, grp, heads):
    i = pl.program_id(2)
    qts = [qt_ref[h] for h in range(heads)]

    def scores(h, n):
        off = pl.multiple_of(n * tq, tq)
        return jnp.dot(kb_ref[h, pl.ds(off, tq), :], qts[h], preferred_element_type=F32)

    def update(h, s, n, m, acc):
        m_new = jnp.maximum(m, jnp.max(s, axis=0, keepdims=True).astype(F32))
        alpha = jnp.exp2(m - m_new)
        p = jnp.exp2(s - m_new.astype(BF16))
        acc = alpha * acc + jnp.dot(vt_ref[h, n], p, preferred_element_type=F32)
        return m_new, acc

    key_i = lax.broadcasted_iota(I32, (tq, tq), 0)
    qry_i = lax.broadcasted_iota(I32, (tq, tq), 1)
    def run_group(blocks, carries):
        s = [[(scores(h, n).astype(BF16) + bias(h).astype(BF16)) if bias is not None else
              jnp.where(key_i <= qry_i, scores(h, n), NEG).astype(BF16) for n, bias in blocks]
             for h in range(heads)]
        carries = list(carries)
        for c, (n, _) in enumerate(blocks):
            for h in range(heads):
                carries[h] = update(h, s[h][c], n, *carries[h])
        return tuple(carries)

    init = (jnp.full((1, tq), NEG, F32), jnp.zeros((HEAD_DIM + V_ONES_ROWS, tq), F32))
    first = [(i, None)] + [(c, functools.partial(lambda h, c: bias_ref[h, c:c + 1, :], c=c))
                           for c in range(grp - 1)]
    carries = run_group(first, (init,) * heads)

    def body(g, carries):
        base = (grp - 1) + g * grp
        blocks = [(base + c, functools.partial(lambda h, n: bias_ref[h, pl.ds(n, 1), :], n=base + c))
                  for c in range(grp)]
        return run_group(blocks, carries)

    n_groups = (jnp.maximum(i - (grp - 1), 0) + grp - 1) // grp
    carries = lax.fori_loop(0, n_groups, body, tuple(carries))
    for h in range(heads):
        _, acc = carries[h]
        out = acc[:HEAD_DIM] / acc[HEAD_DIM:HEAD_DIM + 1]
        o_ref[:, h * HEAD_DIM:(h + 1) * HEAD_DIM] = out.T.astype(o_ref.dtype)


def _moba_prompt(qt, bias, kb, vt, grp=2, heads=8):
    b, n_heads, nb, _, tq = qt.shape
    t = nb * tq
    assert nb % grp == 0
    assert n_heads % heads == 0
    return pl.pallas_call(
        functools.partial(_moba_prompt_kernel, tq=tq, grp=grp, heads=heads),
        grid=(b, n_heads // heads, nb),
        in_specs=[
            pl.BlockSpec((None, heads, None, HEAD_DIM, tq), lambda bi, h, i: (bi, h, i, 0, 0)),
            pl.BlockSpec((None, heads, None, nb, tq), lambda bi, h, i: (bi, h, i, 0, 0)),
            pl.BlockSpec((None, heads, t, HEAD_DIM), lambda bi, h, i: (bi, h, 0, 0),
                         pipeline_mode=pl.Buffered(1)),
            pl.BlockSpec((None, heads, nb, HEAD_DIM + V_ONES_ROWS, tq), lambda bi, h, i: (bi, h, 0, 0, 0),
                         pipeline_mode=pl.Buffered(1)),
        ],
        out_specs=pl.BlockSpec((None, tq, heads * HEAD_DIM), lambda bi, h, i: (bi, i, h)),
        out_shape=jax.ShapeDtypeStruct((b, t, n_heads * HEAD_DIM), BF16),
        compiler_params=_cparams("arbitrary", "arbitrary", "arbitrary"),
        name="moba_prompt",
    )(qt, bias, kb, vt)


def _proj_res_kernel(x_ref, a_ref, w_ref, o_ref):
    o_ref[...] = x_ref[...] + _dot(a_ref[...], w_ref[...])


def _proj_res(x, a, w, tm):
    n, d = x.shape
    w = w.astype(BF16)
    return pl.pallas_call(
        _proj_res_kernel,
        grid=(n // tm,),
        in_specs=[
            pl.BlockSpec((tm, d), lambda i: (i, 0)),
            pl.BlockSpec((tm, a.shape[1]), lambda i: (i, 0)),
            pl.BlockSpec(w.shape, lambda i: (0, 0)),
        ],
        out_specs=pl.BlockSpec((tm, d), lambda i: (i, 0)),
        out_shape=jax.ShapeDtypeStruct((n, d), F32),
        compiler_params=_cparams("arbitrary"),
        name="proj_res",
    )(x, a, w)


def _kvq_sample_kernel(x_ref, gkv_ref, gq_ref, wkv_ref, wq_ref, kn_ref, qn_ref, k_ref, v_ref, q_ref,
                       *, n_heads):
    d = n_heads * HEAD_DIM
    x = x_ref[...]
    kv = _dot(_rms(x, gkv_ref[...]), wkv_ref[...])
    q = _dot(_rms(x, gq_ref[...]), wq_ref[...])
    v_ref[...] = kv[:, d:]
    for h in range(n_heads):
        hs = slice(h * HEAD_DIM, (h + 1) * HEAD_DIM)
        k_ref[:, hs] = _head_rms(kv[:, hs], kn_ref[...])
        q_ref[:, hs] = _head_rms(q[:, hs], qn_ref[...])


def _kvq_sample(x, gkv, gq, w_kv, w_q, kn, qn):
    n, d = x.shape
    sds = jax.ShapeDtypeStruct((n, d), F32)
    return pl.pallas_call(
        functools.partial(_kvq_sample_kernel, n_heads=d // HEAD_DIM),
        out_shape=[sds, sds, sds],
        compiler_params=pltpu.CompilerParams(vmem_limit_bytes=VMEM_LIMIT),
        name="kvq_sample",
    )(x, gkv, gq, w_kv, w_q, kn, qn)


def _sample_blocks_kernel(pt_ref, q_ref, *refs, pages_per_step, pages_per_block):
    del pt_ref
    k_refs = refs[:pages_per_step]
    v_refs = refs[pages_per_step:2 * pages_per_step]
    km_ref, m_ref, l_ref, o_ref = refs[2 * pages_per_step:]
    q = q_ref[...] * QK_SCALE
    n_keys = pages_per_block * k_refs[0].shape[0]
    for c in range(pages_per_step // pages_per_block):
        pages = range(c * pages_per_block, (c + 1) * pages_per_block)
        ks = [k_refs[r][...] for r in pages]
        s = [jnp.sum(kp * q[None], axis=-1, keepdims=True) for kp in ks]
        m = s[0].max(axis=0)
        ksum = ks[0].sum(axis=0)
        for sr, kp in zip(s[1:], ks[1:]):
            m = jnp.maximum(m, sr.max(axis=0))
            ksum = ksum + kp.sum(axis=0)
        l = jnp.zeros_like(m)
        o = jnp.zeros(q.shape, F32)
        for sr, r in zip(s, pages):
            p = jnp.exp(sr - m[None])
            l = l + p.sum(axis=0)
            o = o + (p * v_refs[r][...]).sum(axis=0)
        km_ref[c] = ksum * (1.0 / n_keys)
        m_ref[c] = jnp.broadcast_to(m, q.shape)
        l_ref[c] = jnp.broadcast_to(l, q.shape)
        o_ref[c] = o


def _sample_blocks(q, cache_k, cache_v, page_table, pages_per_step=16):
    n_seq, n_pages = page_table.shape
    _, page, n_heads, hd = cache_k.shape
    ppb = MOBA_BLOCK // page
    assert MOBA_BLOCK % page == 0 and n_pages % pages_per_step == 0 and pages_per_step % ppb == 0
    steps = n_pages // pages_per_step
    bps = pages_per_step // ppb

    def page_spec(c):
        return pl.BlockSpec((None, page, n_heads, hd),
                            lambda s, g, pt: (pt[(s * steps + g) * pages_per_step + c], 0, 0, 0))

    blk_out = pl.BlockSpec((None, bps, n_heads, hd), lambda s, g, pt: (s, g, 0, 0))
    grid_spec = pltpu.PrefetchScalarGridSpec(
        num_scalar_prefetch=1,
        grid=(n_seq, steps),
        in_specs=[pl.BlockSpec((None, n_heads, hd), lambda s, g, pt: (s, 0, 0))]
        + [page_spec(c) for c in range(pages_per_step)] * 2,
        out_specs=[blk_out] * 4,
    )
    sds = jax.ShapeDtypeStruct((n_seq, n_pages // ppb, n_heads, hd), F32)
    return pl.pallas_call(
        functools.partial(_sample_blocks_kernel, pages_per_step=pages_per_step, pages_per_block=ppb),
        grid_spec=grid_spec,
        out_shape=[sds] * 4,
        compiler_params=_cparams("arbitrary", "arbitrary"),
        name="sample_blocks",
    )(page_table.reshape(-1), q, *([cache_k] * pages_per_step), *([cache_v] * pages_per_step))


def _sample_combine_kernel(q_ref, kn_ref, vn_ref, km_ref, m_ref, l_ref, o_ref, out_ref):
    q = q_ref[...]
    qb = q.astype(BF16).astype(F32)
    gate = jnp.sum(km_ref[...].astype(BF16).astype(F32) * qb[None], axis=-1, keepdims=True)
    blk_iota = lax.broadcasted_iota(I32, gate.shape, 0)
    ninf = jnp.float32(-jnp.inf)
    sel = jnp.zeros(gate.shape, jnp.bool_)
    for _ in range(MOBA_TOPK):
        g_max = jnp.max(gate, axis=0, keepdims=True)
        idx = jnp.min(jnp.where(gate == g_max, blk_iota, BIG_IDX), axis=0, keepdims=True)
        hit = blk_iota == idx
        sel = sel | (hit & (g_max > ninf))
        gate = jnp.where(hit, ninf, gate)
    s_new = jnp.sum(q * QK_SCALE * kn_ref[...], axis=-1, keepdims=True)
    mb = m_ref[...]
    m_tot = jnp.maximum(jnp.max(jnp.where(sel, mb, ninf), axis=0), s_new)
    w = jnp.where(sel, jnp.exp(mb - m_tot[None]), 0.0)
    p_new = jnp.exp(s_new - m_tot)
    l = jnp.sum(w * l_ref[...], axis=0) + p_new
    o = jnp.sum(w * o_ref[...], axis=0) + p_new * vn_ref[...]
    out_ref[...] = o / l


def _sample_combine(q, k_new, v_new, km, mb, lb, ob):
    n_seq, nblk, n_heads, hd = km.shape
    row = pl.BlockSpec((None, n_heads, hd), lambda s: (s, 0, 0))
    blk = pl.BlockSpec((None, nblk, n_heads, hd), lambda s: (s, 0, 0, 0))
    return pl.pallas_call(
        _sample_combine_kernel,
        grid=(n_seq,),
        in_specs=[row, row, row, blk, blk, blk, blk],
        out_specs=row,
        out_shape=jax.ShapeDtypeStruct((n_seq, n_heads, hd), F32),
        compiler_params=_cparams("arbitrary"),
        name="sample_combine",
    )(q, k_new, v_new, km, mb, lb, ob)


def kernel(x_prompt, x_sample, state_conv, cache_k, cache_v, page_table, norm_mix, norm_ffn, norm_kv,
           w_conv_in, w_conv, w_conv_out, w_kv, k_norm, w_q, q_norm, w_o, w_router_group,
           b_router_group, w_router_expert, b_router_expert, w_expert_gate, w_expert_up, w_expert_down):
    bp, tp, d = x_prompt.shape
    bs = x_sample.shape[0]
    n_heads = d // HEAD_DIM
    kn = k_norm[None, :]
    qn = q_norm[0][None, :]

    def moe(x_main, x_sample_rows, layer):
        tail = jnp.concatenate([x_sample_rows, jnp.zeros((MOE_TILE - bs, d), F32)], axis=0)
        main, tail = _moe(x_main, tail, layer, norm_ffn[layer][None, :],
                          w_router_group[layer], b_router_group[layer],
                          w_router_expert[layer], b_router_expert[layer],
                          w_expert_gate, w_expert_up, w_expert_down, tm=MOE_TILE, blk=MOE_ROW_BLOCK)
        return main, tail[:bs]

    x1, conv_p = _mixer_prompt(x_prompt, norm_mix[0][None, :], w_conv_in[0], w_conv[0], w_conv_out[0])
    st = state_conv[0]
    x1s, up_s = _mixer_sample(x_sample.reshape(bs, d), st[:, 0, :], st[:, 1, :], norm_mix[0][None, :],
                              w_conv_in[0], w_conv[0], w_conv_out[0])
    conv_s = jnp.stack([st[:, 1, :], up_s], axis=1)[None]
    x2, x2s = moe(x1.reshape(bp * tp, d), x1s, 0)

    k_p, v_p, kb, vt, qt, bias = _kvq_prompt(x2.reshape(bp, tp, d), norm_kv[None, :], norm_mix[1][None, :],
                                             w_kv, w_q[0], kn, qn)
    attn = _moba_prompt(qt, bias, kb, vt)
    x3 = _proj_res(x2, attn.reshape(bp * tp, d), w_o[0], 512)
    k_s, v_s, q_s = _kvq_sample(x2s, norm_kv[None, :], norm_mix[1][None, :], w_kv, w_q[0], kn, qn)
    heads = lambda z: z.reshape(bs, n_heads, HEAD_DIM)
    km, mb, lb, ob = _sample_blocks(heads(q_s), cache_k, cache_v, page_table)
    attn_s = _sample_combine(heads(q_s), heads(k_s), heads(v_s), km, mb, lb, ob)
    x3s = _proj_res(x2s, attn_s.reshape(bs, d), w_o[0], bs)

    y_p, y_s = moe(x3, x3s, 1)
    y_p = y_p.reshape(bp, tp, d)
    y_s = y_s.reshape(bs, 1, d)

    return (y_p, y_s, conv_p[None],
            k_p.reshape(bp, tp, n_heads, HEAD_DIM), v_p.reshape(bp, tp, n_heads, HEAD_DIM),
            conv_s,
            k_s.reshape(bs, 1, n_heads, HEAD_DIM), v_s.reshape(bs, 1, n_heads, HEAD_DIM))
```

```python
import functools

import jax
import jax.numpy as jnp
from jax import lax
from jax.experimental import pallas as pl
from jax.experimental.pallas import tpu as pltpu

F32 = jnp.float32
BF16 = jnp.bfloat16
I32 = jnp.int32

EPS = 1e-6
HEAD_DIM = 128
MOBA_BLOCK = 256
MOBA_TOPK = 3
N_GROUPS = 4
EXPERTS_PER_GROUP = 8
EXPERT_TOPK = 2
CONV_W = 3

LANE = 128
SUBLANE = 8
VMEM_LIMIT = 48 * 1024 * 1024
NEG = -1e30
BIG_IDX = 1 << 20
LOG2_E = 1.4426950408889634
QK_SCALE = HEAD_DIM ** -0.5
QK_SCALE_LOG2 = QK_SCALE * LOG2_E
V_ONES_ROWS = 16


def _cparams(*sem):
    return pltpu.CompilerParams(dimension_semantics=sem, vmem_limit_bytes=VMEM_LIMIT)


def _rms(x, g):
    ms = jnp.mean(x * x, axis=-1, keepdims=True)
    return x * lax.rsqrt(ms + EPS) * g


def _dot(a, b):
    return jnp.dot(a.astype(BF16), b.astype(BF16), preferred_element_type=F32)


def _mixer_prompt_kernel(x_ref, g_ref, win_ref, wc_ref, wout_ref, o_ref, st_ref, buf_ref, *, tm, d):
    @pl.when(pl.program_id(1) == 0)
    def _():
        buf_ref[0:SUBLANE, :] = jnp.zeros((SUBLANE, d), F32)

    x = x_ref[...]
    h = _rms(x, g_ref[...])
    u = _dot(h, win_ref[...])
    bg = u[:, :d]
    up = u[:, d:2 * d] * u[:, 2 * d:]
    buf_ref[SUBLANE:SUBLANE + tm, :] = up
    up1 = buf_ref[SUBLANE - 1:SUBLANE - 1 + tm, :]
    up2 = buf_ref[SUBLANE - 2:SUBLANE - 2 + tm, :]
    wc = wc_ref[...]
    y = wc[0:1] * up2 + wc[1:2] * up1 + wc[2:3] * up
    o_ref[...] = x + _dot(bg * y, wout_ref[...])
    last2 = buf_ref[tm + SUBLANE - 2:tm + SUBLANE, :]
    st_ref[...] = last2
    buf_ref[SUBLANE - 2:SUBLANE, :] = last2


def _mixer_prompt(x, g, w_in, w_c, w_out, tm=256):
    b, t, d = x.shape
    kern = functools.partial(_mixer_prompt_kernel, tm=tm, d=d)
    return pl.pallas_call(
        kern,
        grid=(b, t // tm),
        in_specs=[
            pl.BlockSpec((None, tm, d), lambda i, j: (i, j, 0)),
            pl.BlockSpec((1, d), lambda i, j: (0, 0)),
            pl.BlockSpec((d, 3 * d), lambda i, j: (0, 0)),
            pl.BlockSpec((CONV_W, d), lambda i, j: (0, 0)),
            pl.BlockSpec((d, d), lambda i, j: (0, 0)),
        ],
        out_specs=[
            pl.BlockSpec((None, tm, d), lambda i, j: (i, j, 0)),
            pl.BlockSpec((None, CONV_W - 1, d), lambda i, j: (i, 0, 0)),
        ],
        out_shape=[
            jax.ShapeDtypeStruct((b, t, d), F32),
            jax.ShapeDtypeStruct((b, CONV_W - 1, d), F32),
        ],
        scratch_shapes=[pltpu.VMEM((tm + SUBLANE, d), F32)],
        compiler_params=_cparams("arbitrary", "arbitrary"),
        name="mixer_prompt",
    )(x, g, w_in.astype(BF16), w_c, w_out.astype(BF16))


def _mixer_sample_kernel(x_ref, p0_ref, p1_ref, g_ref, win_ref, wc_ref, wout_ref, o_ref, up_ref, *, d):
    x = x_ref[...]
    h = _rms(x, g_ref[...])
    u = _dot(h, win_ref[...])
    bg = u[:, :d]
    up = u[:, d:2 * d] * u[:, 2 * d:]
    wc = wc_ref[...]
    y = wc[0:1] * p0_ref[...] + wc[1:2] * p1_ref[...] + wc[2:3] * up
    o_ref[...] = x + _dot(bg * y, wout_ref[...])
    up_ref[...] = up


def _mixer_sample(x, p0, p1, g, w_in, w_c, w_out):
    n, d = x.shape
    return pl.pallas_call(
        functools.partial(_mixer_sample_kernel, d=d),
        out_shape=[jax.ShapeDtypeStruct((n, d), F32), jax.ShapeDtypeStruct((n, d), F32)],
        compiler_params=pltpu.CompilerParams(vmem_limit_bytes=VMEM_LIMIT),
        name="mixer_sample",
    )(x, p0, p1, g, w_in, w_c, w_out)


def _token_tile(main_ref, tail_ref, n_main):
    return jnp.where(pl.program_id(0) < n_main, main_ref[...], tail_ref[...])


def _token_specs(tm, d, n_main):
    return [pl.BlockSpec((tm, d), lambda i, *_: (jnp.minimum(i, n_main - 1), 0)),
            pl.BlockSpec((tm, d), lambda i, *_: (jnp.maximum(i - n_main, 0), 0))]


def _router_kernel(xm_ref, xt_ref, g_ref, wr_ref, br_ref, info_ref, cnt_ref, run_ref, *, n_main):
    @pl.when(pl.program_id(0) == 0)
    def _():
        run_ref[...] = jnp.zeros(run_ref.shape, F32)

    xn = _rms(_token_tile(xm_ref, xt_ref, n_main), g_ref[...])
    lg = _dot(xn, wr_ref[...]) + br_ref[...]
    lane = lax.broadcasted_iota(I32, lg.shape, 1)
    ninf = jnp.float32(-jnp.inf)
    is_g = lane < N_GROUPS
    gl = jnp.where(is_g, lg, ninf)
    gmax = jnp.max(gl, axis=-1, keepdims=True)
    gsel = jnp.min(jnp.where(gl == gmax, lane, BIG_IDX), axis=-1, keepdims=True)
    gsum = jnp.sum(jnp.where(is_g, jnp.exp(gl - gmax), 0.0), axis=-1, keepdims=True)
    p_group = 1.0 / gsum
    lo = N_GROUPS + gsel * EXPERTS_PER_GROUP
    in_grp = (lane >= lo) & (lane < lo + EXPERTS_PER_GROUP)
    el = jnp.where(in_grp, lg, ninf)
    e1 = jnp.max(el, axis=-1, keepdims=True)
    i1 = jnp.min(jnp.where(el == e1, lane, BIG_IDX), axis=-1, keepdims=True)
    el2 = jnp.where(lane == i1, ninf, el)
    e2 = jnp.max(el2, axis=-1, keepdims=True)
    i2 = jnp.min(jnp.where(el2 == e2, lane, BIG_IDX), axis=-1, keepdims=True)
    r = jnp.exp(e2 - e1)
    w1 = p_group / (1.0 + r)
    w2 = w1 * r
    eid1 = i1 - N_GROUPS
    eid2 = i2 - N_GROUPS
    hit1 = lane == eid1
    hit2 = lane == eid2
    onehot = jnp.where(hit1 | hit2, 1.0, 0.0)
    tm = onehot.shape[0]
    tri = jnp.where(lax.broadcasted_iota(I32, (tm, tm), 0) > lax.broadcasted_iota(I32, (tm, tm), 1), 1.0, 0.0)
    before = _dot(tri, onehot) + run_ref[0:1, :]
    rank1 = jnp.sum(jnp.where(hit1, before, 0.0), axis=-1, keepdims=True)
    rank2 = jnp.sum(jnp.where(hit2, before, 0.0), axis=-1, keepdims=True)
    total = run_ref[0:1, :] + jnp.sum(onehot, axis=0, keepdims=True)
    run_ref[0:1, :] = total
    cnt_ref[...] = total
    cols = (eid1.astype(F32), eid2.astype(F32), w1, w2, rank1, rank2)
    info = jnp.zeros(lg.shape, F32)
    for c, val in enumerate(cols):
        info = jnp.where(lane == c, val, info)
    info_ref[...] = info


ROUTE_EID, ROUTE_W, ROUTE_RANK = 0, 2, 4
MOE_TILE = 256
MOE_ROUTER_TILE = 512
MOE_ROW_BLOCK = 512


def _router(x_main, x_tail, g, wr, br, tm):
    d = x_main.shape[1]
    n_main = x_main.shape[0] // tm
    n = x_main.shape[0] + x_tail.shape[0]
    return pl.pallas_call(
        functools.partial(_router_kernel, n_main=n_main),
        grid=(n // tm,),
        in_specs=_token_specs(tm, d, n_main) + [
            pl.BlockSpec((1, d), lambda i: (0, 0)),
            pl.BlockSpec((d, LANE), lambda i: (0, 0)),
            pl.BlockSpec((1, LANE), lambda i: (0, 0)),
        ],
        out_specs=[
            pl.BlockSpec((tm, LANE), lambda i: (i, 0)),
            pl.BlockSpec((1, LANE), lambda i: (0, 0)),
        ],
        out_shape=[jax.ShapeDtypeStruct((n, LANE), F32), jax.ShapeDtypeStruct((1, LANE), F32)],
        scratch_shapes=[pltpu.VMEM((SUBLANE, LANE), F32)],
        compiler_params=_cparams("arbitrary"),
        name="router",
    )(x_main, x_tail, g, wr, br)


ISSUE_UNROLL = 8


def _pack_bf16_pairs(x):
    c = x.shape[1] // 2
    bits = lax.bitcast_convert_type(x.astype(BF16).astype(F32), jnp.uint32)
    return (bits[:, :c] >> 16) | (bits[:, c:] & jnp.uint32(0xFFFF0000))


def _unpack_bf16_pairs(u):
    lo = lax.bitcast_convert_type(u << 16, F32)
    hi = lax.bitcast_convert_type(u & jnp.uint32(0xFFFF0000), F32)
    return jnp.concatenate([lo, hi], axis=1).astype(BF16)


def _row_copy(src, src_row, dst, dst_row, sem):
    return pltpu.make_async_copy(src.at[pl.ds(src_row, 1), :], dst.at[pl.ds(dst_row, 1), :], sem)


def _dispatch_kernel(dest_ref, xm_ref, xt_ref, g_ref, zeros_ref, xs_ref, buf, sem, *, tm, n_steps, n_main):
    del zeros_ref
    i = pl.program_id(0)
    slot = i % 2

    def wait_rows(s):
        for _ in range(EXPERT_TOPK):
            pltpu.make_async_copy(buf.at[s], xs_ref.at[pl.ds(0, tm), :], sem.at[s]).wait()

    @pl.when(i >= 2)
    def _():
        wait_rows(slot)

    buf[slot] = _pack_bf16_pairs(_rms(_token_tile(xm_ref, xt_ref, n_main), g_ref[...]))

    def issue(r, carry):
        a = (i * tm + r) * EXPERT_TOPK
        for k in range(EXPERT_TOPK):
            _row_copy(buf.at[slot], r, xs_ref, dest_ref[a + k], sem.at[slot]).start()
        return carry

    lax.fori_loop(0, tm, issue, 0, unroll=ISSUE_UNROLL)

    @pl.when(i == n_steps - 1)
    def _():
        if n_steps > 1:
            wait_rows(1 - slot)
        wait_rows(slot)


def _dispatch(x_main, x_tail, g, dest, p, tm):
    d = x_main.shape[1]
    n_main = x_main.shape[0] // tm
    n_steps = n_main + x_tail.shape[0] // tm
    grid_spec = pltpu.PrefetchScalarGridSpec(
        num_scalar_prefetch=1,
        grid=(n_steps,),
        in_specs=_token_specs(tm, d, n_main) + [
            pl.BlockSpec((1, d), lambda i, dst: (0, 0)),
            pl.BlockSpec(memory_space=pl.ANY),
        ],
        out_specs=pl.BlockSpec(memory_space=pl.ANY),
        scratch_shapes=[pltpu.VMEM((2, tm, d // 2), jnp.uint32), pltpu.SemaphoreType.DMA((2,))],
    )
    return pl.pallas_call(
        functools.partial(_dispatch_kernel, tm=tm, n_steps=n_steps, n_main=n_main),
        grid_spec=grid_spec,
        out_shape=jax.ShapeDtypeStruct((p, d // 2), jnp.uint32),
        input_output_aliases={4: 0},
        compiler_params=_cparams("arbitrary"),
        name="dispatch",
    )(dest, x_main, x_tail, g, jnp.zeros((p, d // 2), jnp.uint32))


def _combine_kernel(dest_ref, xm_ref, xt_ref, info_ref, ys_ref, om_ref, ot_ref, buf, sem, *,
                    tm, n_steps, n_main):
    i = pl.program_id(0)
    slot = i % 2

    def issue_step(step, s):
        def issue(r, carry):
            a = (step * tm + r) * EXPERT_TOPK
            for k in range(EXPERT_TOPK):
                _row_copy(ys_ref, dest_ref[a + k], buf.at[s, k], r, sem.at[s]).start()
            return carry
        lax.fori_loop(0, tm, issue, 0, unroll=ISSUE_UNROLL)

    @pl.when(i == 0)
    def _():
        issue_step(0, 0)

    @pl.when(i + 1 < n_steps)
    def _():
        issue_step(i + 1, 1 - slot)

    for k in range(EXPERT_TOPK):
        pltpu.make_async_copy(ys_ref.at[pl.ds(0, tm), :], buf.at[slot, k], sem.at[slot]).wait()
    out = _token_tile(xm_ref, xt_ref, n_main)
    for k in range(EXPERT_TOPK):
        out = out + info_ref[:, ROUTE_W + k:ROUTE_W + k + 1] * buf[slot, k]

    @pl.when(i < n_main)
    def _():
        om_ref[...] = out

    @pl.when(i >= n_main)
    def _():
        ot_ref[...] = out


def _combine(x_main, x_tail, info, ys, dest, tm):
    d = x_main.shape[1]
    n_main = x_main.shape[0] // tm
    n_steps = n_main + x_tail.shape[0] // tm
    grid_spec = pltpu.PrefetchScalarGridSpec(
        num_scalar_prefetch=1,
        grid=(n_steps,),
        in_specs=_token_specs(tm, d, n_main) + [
            pl.BlockSpec((tm, LANE), lambda i, dst: (i, 0)),
            pl.BlockSpec(memory_space=pl.ANY),
        ],
        out_specs=_token_specs(tm, d, n_main),
        scratch_shapes=[pltpu.VMEM((2, EXPERT_TOPK, tm, d), F32), pltpu.SemaphoreType.DMA((2,))],
    )
    return pl.pallas_call(
        functools.partial(_combine_kernel, tm=tm, n_steps=n_steps, n_main=n_main),
        grid_spec=grid_spec,
        out_shape=[jax.ShapeDtypeStruct(x_main.shape, F32), jax.ShapeDtypeStruct(x_tail.shape, F32)],
        compiler_params=_cparams("arbitrary"),
        name="combine",
    )(dest, x_main, x_tail, info, ys)


def _experts_kernel(be_ref, nu_ref, xs_ref, wg_ref, wu_ref, wd_ref, o_ref, wgb, wub, wdb):
    i = pl.program_id(0)
    e = be_ref[i]
    prev = be_ref[jnp.maximum(i - 1, 0)]

    @pl.when((i == 0) | (e != prev))
    def _():
        wgb[...] = wg_ref[...].astype(BF16)
        wub[...] = wu_ref[...].astype(BF16)
        wdb[...] = wd_ref[...].astype(BF16)

    @pl.when(i < nu_ref[0])
    def _():
        xb = _unpack_bf16_pairs(xs_ref[...])
        h1 = jnp.dot(xb, wgb[...], preferred_element_type=F32)
        h2 = jnp.dot(xb, wub[...], preferred_element_type=F32)
        h = h1 * jax.nn.sigmoid(h1) * h2
        o_ref[...] = jnp.dot(h.astype(BF16), wdb[...], preferred_element_type=F32)

    @pl.when(i >= nu_ref[0])
    def _():
        o_ref[...] = jnp.zeros(o_ref.shape, F32)


def _experts(xs, blk_expert, n_used, wg, wu, wd, layer, blk):
    p = xs.shape[0]
    d, de = wg.shape[-2:]
    n_blk = p // blk
    grid_spec = pltpu.PrefetchScalarGridSpec(
        num_scalar_prefetch=2,
        grid=(n_blk,),
        in_specs=[
            pl.BlockSpec((blk, d // 2), lambda i, be, nu: (jnp.minimum(i, nu[0] - 1), 0)),
            pl.BlockSpec((None, None, d, de), lambda i, be, nu: (layer, be[i], 0, 0)),
            pl.BlockSpec((None, None, d, de), lambda i, be, nu: (layer, be[i], 0, 0)),
            pl.BlockSpec((None, None, de, d), lambda i, be, nu: (layer, be[i], 0, 0)),
        ],
        out_specs=pl.BlockSpec((blk, d), lambda i, be, nu: (i, 0)),
        scratch_shapes=[pltpu.VMEM((d, de), BF16), pltpu.VMEM((d, de), BF16), pltpu.VMEM((de, d), BF16)],
    )
    return pl.pallas_call(
        _experts_kernel,
        grid_spec=grid_spec,
        out_shape=jax.ShapeDtypeStruct((p, d), F32),
        compiler_params=_cparams("arbitrary"),
        name="experts",
    )(blk_expert, n_used, xs, wg, wu, wd)


def _moe(x_main, x_tail, layer, g, w_rg, b_rg, w_re, b_re, wg, wu, wd, *, tm_router, tm, blk):
    d = x_main.shape[1]
    n = x_main.shape[0] + x_tail.shape[0]
    n_exp = wg.shape[1]
    pad = LANE - N_GROUPS - n_exp
    wr = jnp.concatenate([w_rg, w_re, jnp.zeros((d, pad), F32)], axis=1)
    br = jnp.concatenate([b_rg, b_re, jnp.zeros((pad,), F32)])[None, :]
    info, cnt = _router(x_main, x_tail, g, wr, br, tm_router)

    a = n * EXPERT_TOPK
    experts = jnp.arange(n_exp, dtype=I32)
    counts = cnt[0, :n_exp].astype(I32)
    padded = ((counts + blk - 1) // blk) * blk
    pad_end = jnp.cumsum(padded)
    pad_start = pad_end - padded
    eid = info[:, ROUTE_EID:ROUTE_EID + EXPERT_TOPK].astype(I32)
    rank = info[:, ROUTE_RANK:ROUTE_RANK + EXPERT_TOPK].astype(I32)
    start = jnp.sum(jnp.where(eid[..., None] == experts, pad_start, 0), axis=-1)
    dest = (start + rank).reshape(a)
    n_blk = -(-a // blk) + n_exp
    n_used = (pad_end[-1] // blk).astype(I32)
    blk_ids = jnp.minimum(jnp.arange(n_blk, dtype=I32), n_used - 1)
    blk_expert = jnp.sum((blk_ids[:, None] * blk >= pad_end[None, :]).astype(I32), axis=1)
    blk_expert = jnp.minimum(blk_expert, n_exp - 1)
    xs = _dispatch(x_main, x_tail, g, dest, n_blk * blk, tm)
    ys = _experts(xs, blk_expert, n_used.reshape(1), wg, wu, wd, layer, blk)
    return _combine(x_main, x_tail, info, ys, dest, tm)


def _head_rms(x, g):
    return x * lax.rsqrt(jnp.mean(x * x, axis=-1, keepdims=True) + EPS) * g


def _kvq_prompt_kernel(x_ref, gkv_ref, gq_ref, wkv_ref, wq_ref, kn_ref, qn_ref,
                       k_ref, v_ref, kb_ref, vt_ref, qt_ref, bias_ref, km_ref, *, n_heads, nb):
    i = pl.program_id(1)
    d = n_heads * HEAD_DIM

    @pl.when(i == 0)
    def _():
        km_ref[...] = jnp.zeros(km_ref.shape, F32)

    x = x_ref[...]
    kv = _dot(_rms(x, gkv_ref[...]), wkv_ref[...])
    q = _dot(_rms(x, gq_ref[...]), wq_ref[...])
    v_ref[...] = kv[:, d:]
    tq = x.shape[0]
    blk_iota = lax.broadcasted_iota(I32, (nb, tq), 0)
    row_iota = lax.broadcasted_iota(I32, (nb, HEAD_DIM), 0)
    ninf = jnp.float32(-jnp.inf)
    for h in range(n_heads):
        hs = slice(h * HEAD_DIM, (h + 1) * HEAD_DIM)
        kh = _head_rms(kv[:, hs], kn_ref[...])
        k_ref[:, hs] = kh
        kb_ref[h] = kh.astype(BF16)
        vt = kv[:, d + h * HEAD_DIM:d + (h + 1) * HEAD_DIM].T.astype(BF16)
        vt_ref[h] = jnp.concatenate([vt, jnp.ones((V_ONES_ROWS, tq), BF16)], axis=0)
        qt = _head_rms(q[:, hs], qn_ref[...]).T
        qt_ref[h] = (qt * QK_SCALE_LOG2).astype(BF16)
        kmh = km_ref[:, hs]
        gate = _dot(kmh, qt)
        gate = jnp.where(blk_iota < i, gate, ninf)
        sel = jnp.zeros((nb, tq), jnp.bool_)
        for _ in range(MOBA_TOPK):
            m = jnp.max(gate, axis=0, keepdims=True)
            idx = jnp.min(jnp.where(gate == m, blk_iota, BIG_IDX), axis=0, keepdims=True)
            hit = blk_iota == idx
            sel = sel | (hit & (m > ninf))
            gate = jnp.where(hit, ninf, gate)
        bias_ref[h] = jnp.where(sel, 0.0, NEG)
        kmean = jnp.mean(kh, axis=0, keepdims=True)
        km_ref[:, hs] = jnp.where(row_iota == i, kmean, kmh)


def _kvq_prompt(x, gkv, gq, w_kv, w_q, kn, qn):
    b, t, d = x.shape
    n_heads = d // HEAD_DIM
    tq = MOBA_BLOCK
    nb = t // tq
    kern = functools.partial(_kvq_prompt_kernel, n_heads=n_heads, nb=nb)
    const = lambda i, j: (0, 0)
    return pl.pallas_call(
        kern,
        grid=(b, nb),
        in_specs=[
            pl.BlockSpec((None, tq, d), lambda i, j: (i, j, 0)),
            pl.BlockSpec((1, d), const),
            pl.BlockSpec((1, d), const),
            pl.BlockSpec((d, 2 * d), const),
            pl.BlockSpec((d, d), const),
            pl.BlockSpec((1, HEAD_DIM), const),
            pl.BlockSpec((1, HEAD_DIM), const),
        ],
        out_specs=[
            pl.BlockSpec((None, tq, d), lambda i, j: (i, j, 0)),
            pl.BlockSpec((None, tq, d), lambda i, j: (i, j, 0)),
            pl.BlockSpec((None, n_heads, tq, HEAD_DIM), lambda i, j: (i, 0, j, 0)),
            pl.BlockSpec((None, n_heads, None, HEAD_DIM + V_ONES_ROWS, tq), lambda i, j: (i, 0, j, 0, 0)),
            pl.BlockSpec((None, n_heads, None, HEAD_DIM, tq), lambda i, j: (i, 0, j, 0, 0)),
            pl.BlockSpec((None, n_heads, None, nb, tq), lambda i, j: (i, 0, j, 0, 0)),
        ],
        out_shape=[
            jax.ShapeDtypeStruct((b, t, d), F32),
            jax.ShapeDtypeStruct((b, t, d), F32),
            jax.ShapeDtypeStruct((b, n_heads, t, HEAD_DIM), BF16),
            jax.ShapeDtypeStruct((b, n_heads, nb, HEAD_DIM + V_ONES_ROWS, tq), BF16),
            jax.ShapeDtypeStruct((b, n_heads, nb, HEAD_DIM, tq), BF16),
            jax.ShapeDtypeStruct((b, n_heads, nb, nb, tq), F32),
        ],
        scratch_shapes=[pltpu.VMEM((nb, d), F32)],
        compiler_params=_cparams("arbitrary", "arbitrary"),
        name="kvq_prompt",
    )(x, gkv, gq, w_kv.astype(BF16), w_q.astype(BF16), kn, qn)


def _moba_prompt_kernel(qt_ref, bias_ref, kb_ref, vt_ref, o_ref, *, tq, grp, heads):
    i = pl.program_id(2)
    qts = [qt_ref[h] for h in range(heads)]

    def scores(h, n):
        off = pl.multiple_of(n * tq, tq)
        return jnp.dot(kb_ref[h, pl.ds(off, tq), :], qts[h], preferred_element_type=F32)

    def update(h, s, n, m, acc):
        m_new = jnp.maximum(m, jnp.max(s, axis=0, keepdims=True).astype(F32))
        alpha = jnp.exp2(m - m_new)
        p = jnp.exp2(s - m_new.astype(BF16))
        acc = alpha * acc + jnp.dot(vt_ref[h, n], p, preferred_element_type=F32)
        return m_new, acc

    key_i = lax.broadcasted_iota(I32, (tq, tq), 0)
    qry_i = lax.broadcasted_iota(I32, (tq, tq), 1)
    def run_group(blocks, carries):
        s = [[(scores(h, n).astype(BF16) + bias(h).astype(BF16)) if bias is not None else
              jnp.where(key_i <= qry_i, scores(h, n), NEG).astype(BF16) for n, bias in blocks]
             for h in range(heads)]
        carries = list(carries)
        for c, (n, _) in enumerate(blocks):
            for h in range(heads):
                carries[h] = update(h, s[h][c], n, *carries[h])
        return tuple(carries)

    init = (jnp.full((1, tq), NEG, F32), jnp.zeros((HEAD_DIM + V_ONES_ROWS, tq), F32))
    first = [(i, None)] + [(c, functools.partial(lambda h, c: bias_ref[h, c:c + 1, :], c=c))
                           for c in range(grp - 1)]
    carries = run_group(first, (init,) * heads)

    def body(g, carries):
        base = (grp - 1) + g * grp
        blocks = [(base + c, functools.partial(lambda h, n: bias_ref[h, pl.ds(n, 1), :], n=base + c))
                  for c in range(grp)]
        return run_group(blocks, carries)

    n_groups = (jnp.maximum(i - (grp - 1), 0) + grp - 1) // grp
    carries = lax.fori_loop(0, n_groups, body, tuple(carries))
    for h in range(heads):
        _, acc = carries[h]
        out = acc[:HEAD_DIM] / acc[HEAD_DIM:HEAD_DIM + 1]
        o_ref[:, h * HEAD_DIM:(h + 1) * HEAD_DIM] = out.T.astype(o_ref.dtype)


def _moba_prompt(qt, bias, kb, vt, grp=4, heads=8):
    b, n_heads, nb, _, tq = qt.shape
    t = nb * tq
    assert nb % grp == 0
    assert n_heads % heads == 0
    return pl.pallas_call(
        functools.partial(_moba_prompt_kernel, tq=tq, grp=grp, heads=heads),
        grid=(b, n_heads // heads, nb),
        in_specs=[
            pl.BlockSpec((None, heads, None, HEAD_DIM, tq), lambda bi, h, i: (bi, h, i, 0, 0)),
            pl.BlockSpec((None, heads, None, nb, tq), lambda bi, h, i: (bi, h, i, 0, 0)),
            pl.BlockSpec((None, heads, t, HEAD_DIM), lambda bi, h, i: (bi, h, 0, 0),
                         pipeline_mode=pl.Buffered(1)),
            pl.BlockSpec((None, heads, nb, HEAD_DIM + V_ONES_ROWS, tq), lambda bi, h, i: (bi, h, 0, 0, 0),
                         pipeline_mode=pl.Buffered(1)),
        ],
        out_specs=pl.BlockSpec((None, tq, heads * HEAD_DIM), lambda bi, h, i: (bi, i, h)),
        out_shape=jax.ShapeDtypeStruct((b, t, n_heads * HEAD_DIM), BF16),
        compiler_params=_cparams("arbitrary", "arbitrary", "arbitrary"),
        name="moba_prompt",
    )(qt, bias, kb, vt)


def _proj_res_kernel(x_ref, a_ref, w_ref, o_ref):
    o_ref[...] = x_ref[...] + _dot(a_ref[...], w_ref[...])


def _proj_res(x, a, w, tm):
    n, d = x.shape
    w = w.astype(BF16)
    return pl.pallas_call(
        _proj_res_kernel,
        grid=(n // tm,),
        in_specs=[
            pl.BlockSpec((tm, d), lambda i: (i, 0)),
            pl.BlockSpec((tm, a.shape[1]), lambda i: (i, 0)),
            pl.BlockSpec(w.shape, lambda i: (0, 0)),
        ],
        out_specs=pl.BlockSpec((tm, d), lambda i: (i, 0)),
        out_shape=jax.ShapeDtypeStruct((n, d), F32),
        compiler_params=_cparams("arbitrary"),
        name="proj_res",
    )(x, a, w)


def _kvq_sample_kernel(x_ref, gkv_ref, gq_ref, wkv_ref, wq_ref, kn_ref, qn_ref, k_ref, v_ref, q_ref,
                       *, n_heads):
    d = n_heads * HEAD_DIM
    x = x_ref[...]
    kv = _dot(_rms(x, gkv_ref[...]), wkv_ref[...])
    q = _dot(_rms(x, gq_ref[...]), wq_ref[...])
    v_ref[...] = kv[:, d:]
    for h in range(n_heads):
        hs = slice(h * HEAD_DIM, (h + 1) * HEAD_DIM)
        k_ref[:, hs] = _head_rms(kv[:, hs], kn_ref[...])
        q_ref[:, hs] = _head_rms(q[:, hs], qn_ref[...])


def _kvq_sample(x, gkv, gq, w_kv, w_q, kn, qn):
    n, d = x.shape
    sds = jax.ShapeDtypeStruct((n, d), F32)
    return pl.pallas_call(
        functools.partial(_kvq_sample_kernel, n_heads=d // HEAD_DIM),
        out_shape=[sds, sds, sds],
        compiler_params=pltpu.CompilerParams(vmem_limit_bytes=VMEM_LIMIT),
        name="kvq_sample",
    )(x, gkv, gq, w_kv, w_q, kn, qn)


def _sample_blocks_kernel(pt_ref, q_ref, *refs, pages_per_step, pages_per_block):
    del pt_ref
    k_refs = refs[:pages_per_step]
    v_refs = refs[pages_per_step:2 * pages_per_step]
    km_ref, m_ref, l_ref, o_ref = refs[2 * pages_per_step:]
    q = q_ref[...] * QK_SCALE
    n_keys = pages_per_block * k_refs[0].shape[0]
    for c in range(pages_per_step // pages_per_block):
        pages = range(c * pages_per_block, (c + 1) * pages_per_block)
        ks = [k_refs[r][...] for r in pages]
        s = [jnp.sum(kp * q[None], axis=-1, keepdims=True) for kp in ks]
        m = s[0].max(axis=0)
        ksum = ks[0].sum(axis=0)
        for sr, kp in zip(s[1:], ks[1:]):
            m = jnp.maximum(m, sr.max(axis=0))
            ksum = ksum + kp.sum(axis=0)
        l = jnp.zeros_like(m)
        o = jnp.zeros(q.shape, F32)
        for sr, r in zip(s, pages):
            p = jnp.exp(sr - m[None])
            l = l + p.sum(axis=0)
            o = o + (p * v_refs[r][...]).sum(axis=0)
        km_ref[c] = ksum * (1.0 / n_keys)
        m_ref[c] = jnp.broadcast_to(m, q.shape)
        l_ref[c] = jnp.broadcast_to(l, q.shape)
        o_ref[c] = o


def _sample_blocks(q, cache_k, cache_v, page_table, pages_per_step=16):
    n_seq, n_pages = page_table.shape
    _, page, n_heads, hd = cache_k.shape
    ppb = MOBA_BLOCK // page
    assert MOBA_BLOCK % page == 0 and n_pages % pages_per_step == 0 and pages_per_step % ppb == 0
    steps = n_pages // pages_per_step
    bps = pages_per_step // ppb

    def page_spec(c):
        return pl.BlockSpec((None, page, n_heads, hd),
                            lambda s, g, pt: (pt[(s * steps + g) * pages_per_step + c], 0, 0, 0))

    blk_out = pl.BlockSpec((None, bps, n_heads, hd), lambda s, g, pt: (s, g, 0, 0))
    grid_spec = pltpu.PrefetchScalarGridSpec(
        num_scalar_prefetch=1,
        grid=(n_seq, steps),
        in_specs=[pl.BlockSpec((None, n_heads, hd), lambda s, g, pt: (s, 0, 0))]
        + [page_spec(c) for c in range(pages_per_step)] * 2,
        out_specs=[blk_out] * 4,
    )
    sds = jax.ShapeDtypeStruct((n_seq, n_pages // ppb, n_heads, hd), F32)
    return pl.pallas_call(
        functools.partial(_sample_blocks_kernel, pages_per_step=pages_per_step, pages_per_block=ppb),
        grid_spec=grid_spec,
        out_shape=[sds] * 4,
        compiler_params=_cparams("arbitrary", "arbitrary"),
        name="sample_blocks",
    )(page_table.reshape(-1), q, *([cache_k] * pages_per_step), *([cache_v] * pages_per_step))


def _sample_combine_kernel(q_ref, kn_ref, vn_ref, km_ref, m_ref, l_ref, o_ref, out_ref):
    q = q_ref[...]
    qb = q.astype(BF16).astype(F32)
    gate = jnp.sum(km_ref[...].astype(BF16).astype(F32) * qb[None], axis=-1, keepdims=True)
    blk_iota = lax.broadcasted_iota(I32, gate.shape, 0)
    ninf = jnp.float32(-jnp.inf)
    sel = jnp.zeros(gate.shape, jnp.bool_)
    for _ in range(MOBA_TOPK):
        g_max = jnp.max(gate, axis=0, keepdims=True)
        idx = jnp.min(jnp.where(gate == g_max, blk_iota, BIG_IDX), axis=0, keepdims=True)
        hit = blk_iota == idx
        sel = sel | (hit & (g_max > ninf))
        gate = jnp.where(hit, ninf, gate)
    s_new = jnp.sum(q * QK_SCALE * kn_ref[...], axis=-1, keepdims=True)
    mb = m_ref[...]
    m_tot = jnp.maximum(jnp.max(jnp.where(sel, mb, ninf), axis=0), s_new)
    w = jnp.where(sel, jnp.exp(mb - m_tot[None]), 0.0)
    p_new = jnp.exp(s_new - m_tot)
    l = jnp.sum(w * l_ref[...], axis=0) + p_new
    o = jnp.sum(w * o_ref[...], axis=0) + p_new * vn_ref[...]
    out_ref[...] = o / l


def _sample_combine(q, k_new, v_new, km, mb, lb, ob):
    n_seq, nblk, n_heads, hd = km.shape
    row = pl.BlockSpec((None, n_heads, hd), lambda s: (s, 0, 0))
    blk = pl.BlockSpec((None, nblk, n_heads, hd), lambda s: (s, 0, 0, 0))
    return pl.pallas_call(
        _sample_combine_kernel,
        grid=(n_seq,),
        in_specs=[row, row, row, blk, blk, blk, blk],
        out_specs=row,
        out_shape=jax.ShapeDtypeStruct((n_seq, n_heads, hd), F32),
        compiler_params=_cparams("arbitrary"),
        name="sample_combine",
    )(q, k_new, v_new, km, mb, lb, ob)


def kernel(x_prompt, x_sample, state_conv, cache_k, cache_v, page_table, norm_mix, norm_ffn, norm_kv,
           w_conv_in, w_conv, w_conv_out, w_kv, k_norm, w_q, q_norm, w_o, w_router_group,
           b_router_group, w_router_expert, b_router_expert, w_expert_gate, w_expert_up, w_expert_down):
    bp, tp, d = x_prompt.shape
    bs = x_sample.shape[0]
    n_heads = d // HEAD_DIM
    kn = k_norm[None, :]
    qn = q_norm[0][None, :]

    def moe(x_main, x_sample_rows, layer):
        tail = jnp.concatenate([x_sample_rows, jnp.zeros((MOE_ROUTER_TILE - bs, d), F32)], axis=0)
        main, tail = _moe(x_main, tail, layer, norm_ffn[layer][None, :],
                          w_router_group[layer], b_router_group[layer],
                          w_router_expert[layer], b_router_expert[layer],
                          w_expert_gate, w_expert_up, w_expert_down,
                          tm_router=MOE_ROUTER_TILE, tm=MOE_TILE, blk=MOE_ROW_BLOCK)
        return main, tail[:bs]

    x1, conv_p = _mixer_prompt(x_prompt, norm_mix[0][None, :], w_conv_in[0], w_conv[0], w_conv_out[0])
    st = state_conv[0]
    x1s, up_s = _mixer_sample(x_sample.reshape(bs, d), st[:, 0, :], st[:, 1, :], norm_mix[0][None, :],
                              w_conv_in[0], w_conv[0], w_conv_out[0])
    conv_s = jnp.stack([st[:, 1, :], up_s], axis=1)[None]
    x2, x2s = moe(x1.reshape(bp * tp, d), x1s, 0)

    k_p, v_p, kb, vt, qt, bias = _kvq_prompt(x2.reshape(bp, tp, d), norm_kv[None, :], norm_mix[1][None, :],
                                             w_kv, w_q[0], kn, qn)
    attn = _moba_prompt(qt, bias, kb, vt)
    x3 = _proj_res(x2, attn.reshape(bp * tp, d), w_o[0], 512)
    k_s, v_s, q_s = _kvq_sample(x2s, norm_kv[None, :], norm_mix[1][None, :], w_kv, w_q[0], kn, qn)
    heads = lambda z: z.reshape(bs, n_heads, HEAD_DIM)
    km, mb, lb, ob = _sample_blocks(heads(q_s), cache_k, cache_v, page_table)
    attn_s = _sample_combine(heads(q_s), heads(k_s), heads(v_s), km, mb, lb, ob)
    x3s = _proj_res(x2s, attn_s.reshape(bs, d), w_o[0], bs)

    y_p, y_s = moe(x3, x3s, 1)
    y_p = y_p.reshape(bp, tp, d)
    y_s = y_s.reshape(bs, 1, d)

    return (y_p, y_s, conv_p[None],
            k_p.reshape(bp, tp, n_heads, HEAD_DIM), v_p.reshape(bp, tp, n_heads, HEAD_DIM),
            conv_s,
            k_s.reshape(bs, 1, n_heads, HEAD_DIM), v_s.reshape(bs, 1, n_heads, HEAD_DIM))
```

```python
import functools

import jax
import jax.numpy as jnp
from jax import lax
from jax.experimental import pallas as pl
from jax.experimental.pallas import tpu as pltpu

F32 = jnp.float32
BF16 = jnp.bfloat16
I32 = jnp.int32

EPS = 1e-6
HEAD_DIM = 128
MOBA_BLOCK = 256
MOBA_TOPK = 3
N_GROUPS = 4
EXPERTS_PER_GROUP = 8
EXPERT_TOPK = 2
CONV_W = 3

LANE = 128
SUBLANE = 8
VMEM_LIMIT = 48 * 1024 * 1024
NEG = -1e30
BIG_IDX = 1 << 20
LOG2_E = 1.4426950408889634
QK_SCALE = HEAD_DIM ** -0.5
QK_SCALE_LOG2 = QK_SCALE * LOG2_E
V_ONES_ROWS = 16


def _cparams(*sem):
    return pltpu.CompilerParams(dimension_semantics=sem, vmem_limit_bytes=VMEM_LIMIT)


def _rms(x, g):
    ms = jnp.mean(x * x, axis=-1, keepdims=True)
    return x * lax.rsqrt(ms + EPS) * g


def _dot(a, b):
    return jnp.dot(a.astype(BF16), b.astype(BF16), preferred_element_type=F32)


def _mixer_prompt_kernel(x_ref, g_ref, win_ref, wc_ref, wout_ref, o_ref, st_ref, buf_ref, *, tm, d):
    @pl.when(pl.program_id(1) == 0)
    def _():
        buf_ref[0:SUBLANE, :] = jnp.zeros((SUBLANE, d), F32)

    x = x_ref[...]
    h = _rms(x, g_ref[...])
    u = _dot(h, win_ref[...])
    bg = u[:, :d]
    up = u[:, d:2 * d] * u[:, 2 * d:]
    buf_ref[SUBLANE:SUBLANE + tm, :] = up
    up1 = buf_ref[SUBLANE - 1:SUBLANE - 1 + tm, :]
    up2 = buf_ref[SUBLANE - 2:SUBLANE - 2 + tm, :]
    wc = wc_ref[...]
    y = wc[0:1] * up2 + wc[1:2] * up1 + wc[2:3] * up
    o_ref[...] = x + _dot(bg * y, wout_ref[...])
    last2 = buf_ref[tm + SUBLANE - 2:tm + SUBLANE, :]
    st_ref[...] = last2
    buf_ref[SUBLANE - 2:SUBLANE, :] = last2


def _mixer_prompt(x, g, w_in, w_c, w_out, tm=256):
    b, t, d = x.shape
    kern = functools.partial(_mixer_prompt_kernel, tm=tm, d=d)
    return pl.pallas_call(
        kern,
        grid=(b, t // tm),
        in_specs=[
            pl.BlockSpec((None, tm, d), lambda i, j: (i, j, 0)),
            pl.BlockSpec((1, d), lambda i, j: (0, 0)),
            pl.BlockSpec((d, 3 * d), lambda i, j: (0, 0)),
            pl.BlockSpec((CONV_W, d), lambda i, j: (0, 0)),
            pl.BlockSpec((d, d), lambda i, j: (0, 0)),
        ],
        out_specs=[
            pl.BlockSpec((None, tm, d), lambda i, j: (i, j, 0)),
            pl.BlockSpec((None, CONV_W - 1, d), lambda i, j: (i, 0, 0)),
        ],
        out_shape=[
            jax.ShapeDtypeStruct((b, t, d), F32),
            jax.ShapeDtypeStruct((b, CONV_W - 1, d), F32),
        ],
        scratch_shapes=[pltpu.VMEM((tm + SUBLANE, d), F32)],
        compiler_params=_cparams("arbitrary", "arbitrary"),
        name="mixer_prompt",
    )(x, g, w_in.astype(BF16), w_c, w_out.astype(BF16))


def _mixer_sample_kernel(x_ref, p0_ref, p1_ref, g_ref, win_ref, wc_ref, wout_ref, o_ref, up_ref, *, d):
    x = x_ref[...]
    h = _rms(x, g_ref[...])
    u = _dot(h, win_ref[...])
    bg = u[:, :d]
    up = u[:, d:2 * d] * u[:, 2 * d:]
    wc = wc_ref[...]
    y = wc[0:1] * p0_ref[...] + wc[1:2] * p1_ref[...] + wc[2:3] * up
    o_ref[...] = x + _dot(bg * y, wout_ref[...])
    up_ref[...] = up


def _mixer_sample(x, p0, p1, g, w_in, w_c, w_out):
    n, d = x.shape
    return pl.pallas_call(
        functools.partial(_mixer_sample_kernel, d=d),
        out_shape=[jax.ShapeDtypeStruct((n, d), F32), jax.ShapeDtypeStruct((n, d), F32)],
        compiler_params=pltpu.CompilerParams(vmem_limit_bytes=VMEM_LIMIT),
        name="mixer_sample",
    )(x, p0, p1, g, w_in, w_c, w_out)


def _token_tile(main_ref, tail_ref, n_main):
    return jnp.where(pl.program_id(0) < n_main, main_ref[...], tail_ref[...])


def _token_specs(tm, d, n_main):
    return [pl.BlockSpec((tm, d), lambda i, *_: (jnp.minimum(i, n_main - 1), 0)),
            pl.BlockSpec((tm, d), lambda i, *_: (jnp.maximum(i - n_main, 0), 0))]


def _router_kernel(xm_ref, xt_ref, g_ref, wr_ref, br_ref, info_ref, cnt_ref, run_ref, *, n_main):
    @pl.when(pl.program_id(0) == 0)
    def _():
        run_ref[...] = jnp.zeros(run_ref.shape, F32)

    xn = _rms(_token_tile(xm_ref, xt_ref, n_main), g_ref[...])
    lg = _dot(xn, wr_ref[...]) + br_ref[...]
    lane = lax.broadcasted_iota(I32, lg.shape, 1)
    ninf = jnp.float32(-jnp.inf)
    is_g = lane < N_GROUPS
    gl = jnp.where(is_g, lg, ninf)
    gmax = jnp.max(gl, axis=-1, keepdims=True)
    gsel = jnp.min(jnp.where(gl == gmax, lane, BIG_IDX), axis=-1, keepdims=True)
    gsum = jnp.sum(jnp.where(is_g, jnp.exp(gl - gmax), 0.0), axis=-1, keepdims=True)
    p_group = 1.0 / gsum
    lo = N_GROUPS + gsel * EXPERTS_PER_GROUP
    in_grp = (lane >= lo) & (lane < lo + EXPERTS_PER_GROUP)
    el = jnp.where(in_grp, lg, ninf)
    e1 = jnp.max(el, axis=-1, keepdims=True)
    i1 = jnp.min(jnp.where(el == e1, lane, BIG_IDX), axis=-1, keepdims=True)
    el2 = jnp.where(lane == i1, ninf, el)
    e2 = jnp.max(el2, axis=-1, keepdims=True)
    i2 = jnp.min(jnp.where(el2 == e2, lane, BIG_IDX), axis=-1, keepdims=True)
    r = jnp.exp(e2 - e1)
    w1 = p_group / (1.0 + r)
    w2 = w1 * r
    eid1 = i1 - N_GROUPS
    eid2 = i2 - N_GROUPS
    hit1 = lane == eid1
    hit2 = lane == eid2
    onehot = jnp.where(hit1 | hit2, 1.0, 0.0)
    tm = onehot.shape[0]
    tri = jnp.where(lax.broadcasted_iota(I32, (tm, tm), 0) > lax.broadcasted_iota(I32, (tm, tm), 1), 1.0, 0.0)
    before = _dot(tri, onehot) + run_ref[0:1, :]
    rank1 = jnp.sum(jnp.where(hit1, before, 0.0), axis=-1, keepdims=True)
    rank2 = jnp.sum(jnp.where(hit2, before, 0.0), axis=-1, keepdims=True)
    total = run_ref[0:1, :] + jnp.sum(onehot, axis=0, keepdims=True)
    run_ref[0:1, :] = total
    cnt_ref[...] = total
    cols = (eid1.astype(F32), eid2.astype(F32), w1, w2, rank1, rank2)
    info = jnp.zeros(lg.shape, F32)
    for c, val in enumerate(cols):
        info = jnp.where(lane == c, val, info)
    info_ref[...] = info


ROUTE_EID, ROUTE_W, ROUTE_RANK = 0, 2, 4
MOE_TILE = 256
MOE_ROUTER_TILE = 512
MOE_ROW_BLOCK = 512


def _router(x_main, x_tail, g, wr, br, tm):
    d = x_main.shape[1]
    n_main = x_main.shape[0] // tm
    n = x_main.shape[0] + x_tail.shape[0]
    return pl.pallas_call(
        functools.partial(_router_kernel, n_main=n_main),
        grid=(n // tm,),
        in_specs=_token_specs(tm, d, n_main) + [
            pl.BlockSpec((1, d), lambda i: (0, 0)),
            pl.BlockSpec((d, LANE), lambda i: (0, 0)),
            pl.BlockSpec((1, LANE), lambda i: (0, 0)),
        ],
        out_specs=[
            pl.BlockSpec((tm, LANE), lambda i: (i, 0)),
            pl.BlockSpec((1, LANE), lambda i: (0, 0)),
        ],
        out_shape=[jax.ShapeDtypeStruct((n, LANE), F32), jax.ShapeDtypeStruct((1, LANE), F32)],
        scratch_shapes=[pltpu.VMEM((SUBLANE, LANE), F32)],
        compiler_params=_cparams("arbitrary"),
        name="router",
    )(x_main, x_tail, g, wr, br)


ISSUE_UNROLL = 8


def _pack_bf16_pairs(x):
    c = x.shape[1] // 2
    bits = lax.bitcast_convert_type(x.astype(BF16).astype(F32), jnp.uint32)
    return (bits[:, :c] >> 16) | (bits[:, c:] & jnp.uint32(0xFFFF0000))


def _unpack_bf16_pairs(u):
    lo = lax.bitcast_convert_type(u << 16, F32)
    hi = lax.bitcast_convert_type(u & jnp.uint32(0xFFFF0000), F32)
    return jnp.concatenate([lo, hi], axis=1).astype(BF16)


def _row_copy(src, src_row, dst, dst_row, sem):
    return pltpu.make_async_copy(src.at[pl.ds(src_row, 1), :], dst.at[pl.ds(dst_row, 1), :], sem)


def _dispatch_kernel(dest_ref, xm_ref, xt_ref, g_ref, zeros_ref, xs_ref, buf, sem, *, tm, n_steps, n_main):
    del zeros_ref
    i = pl.program_id(0)
    slot = i % 2

    def wait_rows(s):
        for _ in range(EXPERT_TOPK):
            pltpu.make_async_copy(buf.at[s], xs_ref.at[pl.ds(0, tm), :], sem.at[s]).wait()

    @pl.when(i >= 2)
    def _():
        wait_rows(slot)

    buf[slot] = _pack_bf16_pairs(_rms(_token_tile(xm_ref, xt_ref, n_main), g_ref[...]))

    def issue(r, carry):
        a = (i * tm + r) * EXPERT_TOPK
        for k in range(EXPERT_TOPK):
            _row_copy(buf.at[slot], r, xs_ref, dest_ref[a + k], sem.at[slot]).start(priority=k % 2)
        return carry

    lax.fori_loop(0, tm, issue, 0, unroll=ISSUE_UNROLL)

    @pl.when(i == n_steps - 1)
    def _():
        if n_steps > 1:
            wait_rows(1 - slot)
        wait_rows(slot)


def _dispatch(x_main, x_tail, g, dest, p, tm):
    d = x_main.shape[1]
    n_main = x_main.shape[0] // tm
    n_steps = n_main + x_tail.shape[0] // tm
    grid_spec = pltpu.PrefetchScalarGridSpec(
        num_scalar_prefetch=1,
        grid=(n_steps,),
        in_specs=_token_specs(tm, d, n_main) + [
            pl.BlockSpec((1, d), lambda i, dst: (0, 0)),
            pl.BlockSpec(memory_space=pl.ANY),
        ],
        out_specs=pl.BlockSpec(memory_space=pl.ANY),
        scratch_shapes=[pltpu.VMEM((2, tm, d // 2), jnp.uint32), pltpu.SemaphoreType.DMA((2,))],
    )
    return pl.pallas_call(
        functools.partial(_dispatch_kernel, tm=tm, n_steps=n_steps, n_main=n_main),
        grid_spec=grid_spec,
        out_shape=jax.ShapeDtypeStruct((p, d // 2), jnp.uint32),
        input_output_aliases={4: 0},
        compiler_params=_cparams("arbitrary"),
        name="dispatch",
    )(dest, x_main, x_tail, g, jnp.zeros((p, d // 2), jnp.uint32))


def _combine_kernel(dest_ref, xm_ref, xt_ref, info_ref, ys_ref, om_ref, ot_ref, buf, sem, *,
                    tm, n_steps, n_main):
    i = pl.program_id(0)
    slot = i % 2

    def issue_step(step, s):
        def issue(r, carry):
            a = (step * tm + r) * EXPERT_TOPK
            for k in range(EXPERT_TOPK):
                _row_copy(ys_ref, dest_ref[a + k], buf.at[s, k], r, sem.at[s]).start(priority=k % 2)
            return carry
        lax.fori_loop(0, tm, issue, 0, unroll=ISSUE_UNROLL)

    @pl.when(i == 0)
    def _():
        issue_step(0, 0)

    @pl.when(i + 1 < n_steps)
    def _():
        issue_step(i + 1, 1 - slot)

    for k in range(EXPERT_TOPK):
        pltpu.make_async_copy(ys_ref.at[pl.ds(0, tm), :], buf.at[slot, k], sem.at[slot]).wait()
    out = _token_tile(xm_ref, xt_ref, n_main)
    for k in range(EXPERT_TOPK):
        out = out + info_ref[:, ROUTE_W + k:ROUTE_W + k + 1] * buf[slot, k]

    @pl.when(i < n_main)
    def _():
        om_ref[...] = out

    @pl.when(i >= n_main)
    def _():
        ot_ref[...] = out


def _combine(x_main, x_tail, info, ys, dest, tm):
    d = x_main.shape[1]
    n_main = x_main.shape[0] // tm
    n_steps = n_main + x_tail.shape[0] // tm
    grid_spec = pltpu.PrefetchScalarGridSpec(
        num_scalar_prefetch=1,
        grid=(n_steps,),
        in_specs=_token_specs(tm, d, n_main) + [
            pl.BlockSpec((tm, LANE), lambda i, dst: (i, 0)),
            pl.BlockSpec(memory_space=pl.ANY),
        ],
        out_specs=_token_specs(tm, d, n_main),
        scratch_shapes=[pltpu.VMEM((2, EXPERT_TOPK, tm, d), F32), pltpu.SemaphoreType.DMA((2,))],
    )
    return pl.pallas_call(
        functools.partial(_combine_kernel, tm=tm, n_steps=n_steps, n_main=n_main),
        grid_spec=grid_spec,
        out_shape=[jax.ShapeDtypeStruct(x_main.shape, F32), jax.ShapeDtypeStruct(x_tail.shape, F32)],
        compiler_params=_cparams("arbitrary"),
        name="combine",
    )(dest, x_main, x_tail, info, ys)


def _experts_kernel(be_ref, nu_ref, xs_ref, wg_ref, wu_ref, wd_ref, o_ref, wgb, wub, wdb):
    i = pl.program_id(0)
    e = be_ref[i]
    prev = be_ref[jnp.maximum(i - 1, 0)]

    @pl.when((i == 0) | (e != prev))
    def _():
        wgb[...] = wg_ref[...].astype(BF16)
        wub[...] = wu_ref[...].astype(BF16)
        wdb[...] = wd_ref[...].astype(BF16)

    @pl.when(i < nu_ref[0])
    def _():
        xb = _unpack_bf16_pairs(xs_ref[...])
        h1 = jnp.dot(xb, wgb[...], preferred_element_type=F32)
        h2 = jnp.dot(xb, wub[...], preferred_element_type=F32)
        h = h1 * jax.nn.sigmoid(h1) * h2
        o_ref[...] = jnp.dot(h.astype(BF16), wdb[...], preferred_element_type=F32)

    @pl.when(i >= nu_ref[0])
    def _():
        o_ref[...] = jnp.zeros(o_ref.shape, F32)


def _experts(xs, blk_expert, n_used, wg, wu, wd, layer, blk):
    p = xs.shape[0]
    d, de = wg.shape[-2:]
    n_blk = p // blk
    grid_spec = pltpu.PrefetchScalarGridSpec(
        num_scalar_prefetch=2,
        grid=(n_blk,),
        in_specs=[
            pl.BlockSpec((blk, d // 2), lambda i, be, nu: (jnp.minimum(i, nu[0] - 1), 0)),
            pl.BlockSpec((None, None, d, de), lambda i, be, nu: (layer, be[i], 0, 0)),
            pl.BlockSpec((None, None, d, de), lambda i, be, nu: (layer, be[i], 0, 0)),
            pl.BlockSpec((None, None, de, d), lambda i, be, nu: (layer, be[i], 0, 0)),
        ],
        out_specs=pl.BlockSpec((blk, d), lambda i, be, nu: (i, 0)),
        scratch_shapes=[pltpu.VMEM((d, de), BF16), pltpu.VMEM((d, de), BF16), pltpu.VMEM((de, d), BF16)],
    )
    return pl.pallas_call(
        _experts_kernel,
        grid_spec=grid_spec,
        out_shape=jax.ShapeDtypeStruct((p, d), F32),
        compiler_params=_cparams("arbitrary"),
        name="experts",
    )(blk_expert, n_used, xs, wg, wu, wd)


def _moe_experts(x_main, x_tail, layer, g, w_rg, b_rg, w_re, b_re, wg, wu, wd, *, tm_router, tm, blk):
    d = x_main.shape[1]
    n = x_main.shape[0] + x_tail.shape[0]
    n_exp = wg.shape[1]
    pad = LANE - N_GROUPS - n_exp
    wr = jnp.concatenate([w_rg, w_re, jnp.zeros((d, pad), F32)], axis=1)
    br = jnp.concatenate([b_rg, b_re, jnp.zeros((pad,), F32)])[None, :]
    info, cnt = _router(x_main, x_tail, g, wr, br, tm_router)

    a = n * EXPERT_TOPK
    experts = jnp.arange(n_exp, dtype=I32)
    counts = cnt[0, :n_exp].astype(I32)
    padded = ((counts + blk - 1) // blk) * blk
    pad_end = jnp.cumsum(padded)
    pad_start = pad_end - padded
    eid = info[:, ROUTE_EID:ROUTE_EID + EXPERT_TOPK].astype(I32)
    rank = info[:, ROUTE_RANK:ROUTE_RANK + EXPERT_TOPK].astype(I32)
    start = jnp.sum(jnp.where(eid[..., None] == experts, pad_start, 0), axis=-1)
    dest = (start + rank).reshape(a)
    n_blk = -(-a // blk) + n_exp
    n_used = (pad_end[-1] // blk).astype(I32)
    blk_ids = jnp.minimum(jnp.arange(n_blk, dtype=I32), n_used - 1)
    blk_expert = jnp.sum((blk_ids[:, None] * blk >= pad_end[None, :]).astype(I32), axis=1)
    blk_expert = jnp.minimum(blk_expert, n_exp - 1)
    xs = _dispatch(x_main, x_tail, g, dest, n_blk * blk, tm)
    ys = _experts(xs, blk_expert, n_used.reshape(1), wg, wu, wd, layer, blk)
    return info, ys, dest


def _head_rms(x, g):
    return x * lax.rsqrt(jnp.mean(x * x, axis=-1, keepdims=True) + EPS) * g


def _kvq_prompt_kernel(dest_ref, x_ref, info_ref, ys_ref, gkv_ref, gq_ref, wkv_ref, wq_ref, kn_ref, qn_ref,
                       x2_ref, k_ref, v_ref, kb_ref, vt_ref, qt_ref, bias_ref, km_ref, buf, sem,
                       *, n_heads, nb, n_steps):
    i = pl.program_id(1)
    lin = pl.program_id(0) * nb + i
    slot = lin % 2
    d = n_heads * HEAD_DIM
    tq = x_ref.shape[0]

    def issue_row(step, s, r):
        a = (step * tq + r) * EXPERT_TOPK
        for k in range(EXPERT_TOPK):
            _row_copy(ys_ref, dest_ref[a + k], buf.at[s, k], r, sem.at[s]).start(priority=k % 2)

    def wait_rows(s):
        for k in range(EXPERT_TOPK):
            pltpu.make_async_copy(ys_ref.at[pl.ds(0, tq), :], buf.at[s, k], sem.at[s]).wait()

    @pl.when(lin == 0)
    def _():
        lax.fori_loop(0, tq, lambda r, c: (issue_row(0, 0, r), c)[1], 0, unroll=ISSUE_UNROLL)

    @pl.when(i == 0)
    def _():
        km_ref[...] = jnp.zeros(km_ref.shape, F32)

    wait_rows(slot)
    x = x_ref[...]
    for k in range(EXPERT_TOPK):
        x = x + info_ref[:, ROUTE_W + k:ROUTE_W + k + 1] * buf[slot, k]
    x2_ref[...] = x
    nxt = jnp.minimum(lin + 1, n_steps - 1)
    rows_per_head = tq // n_heads

    kv = _dot(_rms(x, gkv_ref[...]), wkv_ref[...])
    q = _dot(_rms(x, gq_ref[...]), wq_ref[...])
    v_ref[...] = kv[:, d:]
    tq = x.shape[0]
    blk_iota = lax.broadcasted_iota(I32, (nb, tq), 0)
    row_iota = lax.broadcasted_iota(I32, (nb, HEAD_DIM), 0)
    ninf = jnp.float32(-jnp.inf)
    for h in range(n_heads):
        for r in range(h * rows_per_head, (h + 1) * rows_per_head):
            issue_row(nxt, 1 - slot, r)
        hs = slice(h * HEAD_DIM, (h + 1) * HEAD_DIM)
        kh = _head_rms(kv[:, hs], kn_ref[...])
        k_ref[:, hs] = kh
        kb_ref[h] = kh.astype(BF16)
        vt = kv[:, d + h * HEAD_DIM:d + (h + 1) * HEAD_DIM].T.astype(BF16)
        vt_ref[h] = jnp.concatenate([vt, jnp.ones((V_ONES_ROWS, tq), BF16)], axis=0)
        qt = _head_rms(q[:, hs], qn_ref[...]).T
        qt_ref[h] = (qt * QK_SCALE_LOG2).astype(BF16)
        kmh = km_ref[:, hs]
        gate = _dot(kmh, qt)
        gate = jnp.where(blk_iota < i, gate, ninf)
        sel = jnp.zeros((nb, tq), jnp.bool_)
        for _ in range(MOBA_TOPK):
            m = jnp.max(gate, axis=0, keepdims=True)
            idx = jnp.min(jnp.where(gate == m, blk_iota, BIG_IDX), axis=0, keepdims=True)
            hit = blk_iota == idx
            sel = sel | (hit & (m > ninf))
            gate = jnp.where(hit, ninf, gate)
        bias_ref[h] = jnp.where(sel, 0.0, NEG)
        kmean = jnp.mean(kh, axis=0, keepdims=True)
        km_ref[:, hs] = jnp.where(row_iota == i, kmean, kmh)

    @pl.when(lin == n_steps - 1)
    def _():
        wait_rows(1 - slot)


def _kvq_prompt(x, info, ys, dest, gkv, gq, w_kv, w_q, kn, qn):
    b, t, d = x.shape
    n_heads = d // HEAD_DIM
    tq = MOBA_BLOCK
    nb = t // tq
    kern = functools.partial(_kvq_prompt_kernel, n_heads=n_heads, nb=nb, n_steps=b * nb)
    const = lambda i, j, dst: (0, 0)
    tile = pl.BlockSpec((None, tq, d), lambda i, j, dst: (i, j, 0))
    head_t = lambda rows: pl.BlockSpec((None, n_heads, None, rows, tq), lambda i, j, dst: (i, 0, j, 0, 0))
    grid_spec = pltpu.PrefetchScalarGridSpec(
        num_scalar_prefetch=1,
        grid=(b, nb),
        in_specs=[
            tile,
            pl.BlockSpec((tq, LANE), lambda i, j, dst: (i * nb + j, 0)),
            pl.BlockSpec(memory_space=pl.ANY),
            pl.BlockSpec((1, d), const),
            pl.BlockSpec((1, d), const),
            pl.BlockSpec((d, 2 * d), const),
            pl.BlockSpec((d, d), const),
            pl.BlockSpec((1, HEAD_DIM), const),
            pl.BlockSpec((1, HEAD_DIM), const),
        ],
        out_specs=[
            tile,
            tile,
            tile,
            pl.BlockSpec((None, n_heads, tq, HEAD_DIM), lambda i, j, dst: (i, 0, j, 0)),
            head_t(HEAD_DIM + V_ONES_ROWS),
            head_t(HEAD_DIM),
            head_t(nb),
        ],
        scratch_shapes=[pltpu.VMEM((nb, d), F32), pltpu.VMEM((2, EXPERT_TOPK, tq, d), F32),
                        pltpu.SemaphoreType.DMA((2,))],
    )
    return pl.pallas_call(
        kern,
        grid_spec=grid_spec,
        out_shape=[
            jax.ShapeDtypeStruct((b, t, d), F32),
            jax.ShapeDtypeStruct((b, t, d), F32),
            jax.ShapeDtypeStruct((b, t, d), F32),
            jax.ShapeDtypeStruct((b, n_heads, t, HEAD_DIM), BF16),
            jax.ShapeDtypeStruct((b, n_heads, nb, HEAD_DIM + V_ONES_ROWS, tq), BF16),
            jax.ShapeDtypeStruct((b, n_heads, nb, HEAD_DIM, tq), BF16),
            jax.ShapeDtypeStruct((b, n_heads, nb, nb, tq), F32),
        ],
        compiler_params=_cparams("arbitrary", "arbitrary"),
        name="kvq_prompt",
    )(dest, x, info, ys, gkv, gq, w_kv.astype(BF16), w_q.astype(BF16), kn, qn)


def _moba_prompt_kernel(qt_ref, bias_ref, kb_ref, vt_ref, o_ref, *, tq, grp, heads):
    i = pl.program_id(2)
    qts = [qt_ref[h] for h in range(heads)]

    def scores(h, n):
        off = pl.multiple_of(n * tq, tq)
        return jnp.dot(kb_ref[h, pl.ds(off, tq), :], qts[h], preferred_element_type=F32)

    def update(h, s, n, m, acc):
        m_new = jnp.maximum(m, jnp.max(s, axis=0, keepdims=True).astype(F32))
        alpha = jnp.exp2(m - m_new)
        p = jnp.exp2(s - m_new.astype(BF16))
        acc = alpha * acc + jnp.dot(vt_ref[h, n], p, preferred_element_type=F32)
        return m_new, acc

    key_i = lax.broadcasted_iota(I32, (tq, tq), 0)
    qry_i = lax.broadcasted_iota(I32, (tq, tq), 1)
    def run_group(blocks, carries):
        s = [[(scores(h, n).astype(BF16) + bias(h).astype(BF16)) if bias is not None else
              jnp.where(key_i <= qry_i, scores(h, n), NEG).astype(BF16) for n, bias in blocks]
             for h in range(heads)]
        carries = list(carries)
        for c, (n, _) in enumerate(blocks):
            for h in range(heads):
                carries[h] = update(h, s[h][c], n, *carries[h])
        return tuple(carries)

    init = (jnp.full((1, tq), NEG, F32), jnp.zeros((HEAD_DIM + V_ONES_ROWS, tq), F32))
    first = [(i, None)] + [(c, functools.partial(lambda h, c: bias_ref[h, c:c + 1, :], c=c))
                           for c in range(grp - 1)]
    carries = run_group(first, (init,) * heads)

    def body(g, carries):
        base = (grp - 1) + g * grp
        blocks = [(base + c, functools.partial(lambda h, n: bias_ref[h, pl.ds(n, 1), :], n=base + c))
                  for c in range(grp)]
        return run_group(blocks, carries)

    n_groups = (jnp.maximum(i - (grp - 1), 0) + grp - 1) // grp
    carries = lax.fori_loop(0, n_groups, body, tuple(carries))
    for h in range(heads):
        _, acc = carries[h]
        out = acc[:HEAD_DIM] / acc[HEAD_DIM:HEAD_DIM + 1]
        o_ref[:, h * HEAD_DIM:(h + 1) * HEAD_DIM] = out.T.astype(o_ref.dtype)


def _moba_prompt(qt, bias, kb, vt, grp=4, heads=8):
    b, n_heads, nb, _, tq = qt.shape
    t = nb * tq
    assert nb % grp == 0
    assert n_heads % heads == 0
    return pl.pallas_call(
        functools.partial(_moba_prompt_kernel, tq=tq, grp=grp, heads=heads),
        grid=(b, n_heads // heads, nb),
        in_specs=[
            pl.BlockSpec((None, heads, None, HEAD_DIM, tq), lambda bi, h, i: (bi, h, i, 0, 0)),
            pl.BlockSpec((None, heads, None, nb, tq), lambda bi, h, i: (bi, h, i, 0, 0)),
            pl.BlockSpec((None, heads, t, HEAD_DIM), lambda bi, h, i: (bi, h, 0, 0),
                         pipeline_mode=pl.Buffered(1)),
            pl.BlockSpec((None, heads, nb, HEAD_DIM + V_ONES_ROWS, tq), lambda bi, h, i: (bi, h, 0, 0, 0),
                         pipeline_mode=pl.Buffered(1)),
        ],
        out_specs=pl.BlockSpec((None, tq, heads * HEAD_DIM), lambda bi, h, i: (bi, i, h)),
        out_shape=jax.ShapeDtypeStruct((b, t, n_heads * HEAD_DIM), BF16),
        compiler_params=_cparams("arbitrary", "arbitrary", "arbitrary"),
        name="moba_prompt",
    )(qt, bias, kb, vt)


def _proj_res_kernel(x_ref, a_ref, w_ref, o_ref):
    o_ref[...] = x_ref[...] + _dot(a_ref[...], w_ref[...])


def _proj_res(x, a, w, tm):
    n, d = x.shape
    w = w.astype(BF16)
    return pl.pallas_call(
        _proj_res_kernel,
        grid=(n // tm,),
        in_specs=[
            pl.BlockSpec((tm, d), lambda i: (i, 0)),
            pl.BlockSpec((tm, a.shape[1]), lambda i: (i, 0)),
            pl.BlockSpec(w.shape, lambda i: (0, 0)),
        ],
        out_specs=pl.BlockSpec((tm, d), lambda i: (i, 0)),
        out_shape=jax.ShapeDtypeStruct((n, d), F32),
        compiler_params=_cparams("arbitrary"),
        name="proj_res",
    )(x, a, w)


def _kvq_sample_kernel(x_ref, gkv_ref, gq_ref, wkv_ref, wq_ref, kn_ref, qn_ref, k_ref, v_ref, q_ref,
                       *, n_heads):
    d = n_heads * HEAD_DIM
    x = x_ref[...]
    kv = _dot(_rms(x, gkv_ref[...]), wkv_ref[...])
    q = _dot(_rms(x, gq_ref[...]), wq_ref[...])
    v_ref[...] = kv[:, d:]
    for h in range(n_heads):
        hs = slice(h * HEAD_DIM, (h + 1) * HEAD_DIM)
        k_ref[:, hs] = _head_rms(kv[:, hs], kn_ref[...])
        q_ref[:, hs] = _head_rms(q[:, hs], qn_ref[...])


def _kvq_sample(x, gkv, gq, w_kv, w_q, kn, qn):
    n, d = x.shape
    sds = jax.ShapeDtypeStruct((n, d), F32)
    return pl.pallas_call(
        functools.partial(_kvq_sample_kernel, n_heads=d // HEAD_DIM),
        out_shape=[sds, sds, sds],
        compiler_params=pltpu.CompilerParams(vmem_limit_bytes=VMEM_LIMIT),
        name="kvq_sample",
    )(x, gkv, gq, w_kv, w_q, kn, qn)


def _sample_blocks_kernel(pt_ref, q_ref, *refs, pages_per_step, pages_per_block, chunk):
    del pt_ref
    k_refs = refs[:pages_per_step]
    v_refs = refs[pages_per_step:2 * pages_per_step]
    km_ref, m_ref, l_ref, o_ref = refs[2 * pages_per_step:]
    q = q_ref[...] * QK_SCALE
    page_keys = k_refs[0].shape[0]
    n_blocks = pages_per_step // pages_per_block
    col = jnp.zeros((q.shape[0], 1), F32)
    state = [(col + NEG, col, jnp.zeros(q.shape, F32), jnp.zeros(q.shape, F32))] * n_blocks
    for r in range(pages_per_block):
        for t in range(page_keys // chunk):
            rows = slice(t * chunk, (t + 1) * chunk)
            for c in range(n_blocks):
                m, l, o, ksum = state[c]
                kc = k_refs[c * pages_per_block + r][rows]
                s = jnp.sum(kc * q[None], axis=-1, keepdims=True)
                m_new = jnp.maximum(m, s.max(axis=0))
                alpha = jnp.exp(m - m_new)
                p = jnp.exp(s - m_new[None])
                l = alpha * l + p.sum(axis=0)
                o = alpha * o + (p * v_refs[c * pages_per_block + r][rows]).sum(axis=0)
                state[c] = (m_new, l, o, ksum + kc.sum(axis=0))
    for c in range(n_blocks):
        m, l, o, ksum = state[c]
        km_ref[c] = ksum * (1.0 / (pages_per_block * page_keys))
        m_ref[c] = jnp.broadcast_to(m, q.shape)
        l_ref[c] = jnp.broadcast_to(l, q.shape)
        o_ref[c] = o


def _sample_blocks(q, cache_k, cache_v, page_table, pages_per_step=16, chunk=32):
    n_seq, n_pages = page_table.shape
    _, page, n_heads, hd = cache_k.shape
    ppb = MOBA_BLOCK // page
    assert MOBA_BLOCK % page == 0 and n_pages % pages_per_step == 0 and pages_per_step % ppb == 0
    steps = n_pages // pages_per_step
    bps = pages_per_step // ppb

    def page_spec(c):
        return pl.BlockSpec((None, page, n_heads, hd),
                            lambda s, g, pt: (pt[(s * steps + g) * pages_per_step + c], 0, 0, 0))

    blk_out = pl.BlockSpec((None, bps, n_heads, hd), lambda s, g, pt: (s, g, 0, 0))
    grid_spec = pltpu.PrefetchScalarGridSpec(
        num_scalar_prefetch=1,
        grid=(n_seq, steps),
        in_specs=[pl.BlockSpec((None, n_heads, hd), lambda s, g, pt: (s, 0, 0))]
        + [page_spec(c) for c in range(pages_per_step)] * 2,
        out_specs=[blk_out] * 4,
    )
    sds = jax.ShapeDtypeStruct((n_seq, n_pages // ppb, n_heads, hd), F32)
    return pl.pallas_call(
        functools.partial(_sample_blocks_kernel, pages_per_step=pages_per_step, pages_per_block=ppb,
                          chunk=chunk),
        grid_spec=grid_spec,
        out_shape=[sds] * 4,
        compiler_params=_cparams("arbitrary", "arbitrary"),
        name="sample_blocks",
    )(page_table.reshape(-1), q, *([cache_k] * pages_per_step), *([cache_v] * pages_per_step))


def _sample_combine_kernel(q_ref, kn_ref, vn_ref, km_ref, m_ref, l_ref, o_ref, out_ref):
    q = q_ref[...]
    qb = q.astype(BF16).astype(F32)
    gate = jnp.sum(km_ref[...].astype(BF16).astype(F32) * qb[None], axis=-1, keepdims=True)
    blk_iota = lax.broadcasted_iota(I32, gate.shape, 0)
    ninf = jnp.float32(-jnp.inf)
    sel = jnp.zeros(gate.shape, jnp.bool_)
    for _ in range(MOBA_TOPK):
        g_max = jnp.max(gate, axis=0, keepdims=True)
        idx = jnp.min(jnp.where(gate == g_max, blk_iota, BIG_IDX), axis=0, keepdims=True)
        hit = blk_iota == idx
        sel = sel | (hit & (g_max > ninf))
        gate = jnp.where(hit, ninf, gate)
    s_new = jnp.sum(q * QK_SCALE * kn_ref[...], axis=-1, keepdims=True)
    mb = m_ref[...]
    m_tot = jnp.maximum(jnp.max(jnp.where(sel, mb, ninf), axis=0), s_new)
    w = jnp.where(sel, jnp.exp(mb - m_tot[None]), 0.0)
    p_new = jnp.exp(s_new - m_tot)
    l = jnp.sum(w * l_ref[...], axis=0) + p_new
    o = jnp.sum(w * o_ref[...], axis=0) + p_new * vn_ref[...]
    out_ref[...] = o / l


def _sample_combine(q, k_new, v_new, km, mb, lb, ob):
    n_seq, nblk, n_heads, hd = km.shape
    row = pl.BlockSpec((None, n_heads, hd), lambda s: (s, 0, 0))
    blk = pl.BlockSpec((None, nblk, n_heads, hd), lambda s: (s, 0, 0, 0))
    return pl.pallas_call(
        _sample_combine_kernel,
        grid=(n_seq,),
        in_specs=[row, row, row, blk, blk, blk, blk],
        out_specs=row,
        out_shape=jax.ShapeDtypeStruct((n_seq, n_heads, hd), F32),
        compiler_params=_cparams("arbitrary"),
        name="sample_combine",
    )(q, k_new, v_new, km, mb, lb, ob)


def kernel(x_prompt, x_sample, state_conv, cache_k, cache_v, page_table, norm_mix, norm_ffn, norm_kv,
           w_conv_in, w_conv, w_conv_out, w_kv, k_norm, w_q, q_norm, w_o, w_router_group,
           b_router_group, w_router_expert, b_router_expert, w_expert_gate, w_expert_up, w_expert_down):
    bp, tp, d = x_prompt.shape
    bs = x_sample.shape[0]
    n_heads = d // HEAD_DIM
    kn = k_norm[None, :]
    qn = q_norm[0][None, :]

    n_p = bp * tp
    half = MOE_ROUTER_TILE // 2

    def moe_experts(x_main, x_sample_rows, layer):
        tail = jnp.concatenate([x_sample_rows, jnp.zeros((MOE_ROUTER_TILE - bs, d), F32)], axis=0)
        return tail, _moe_experts(x_main, tail, layer, norm_ffn[layer][None, :],
                                  w_router_group[layer], b_router_group[layer],
                                  w_router_expert[layer], b_router_expert[layer],
                                  w_expert_gate, w_expert_up, w_expert_down,
                                  tm_router=MOE_ROUTER_TILE, tm=MOE_TILE, blk=MOE_ROW_BLOCK)

    x1, conv_p = _mixer_prompt(x_prompt, norm_mix[0][None, :], w_conv_in[0], w_conv[0], w_conv_out[0])
    st = state_conv[0]
    x1s, up_s = _mixer_sample(x_sample.reshape(bs, d), st[:, 0, :], st[:, 1, :], norm_mix[0][None, :],
                              w_conv_in[0], w_conv[0], w_conv_out[0])
    conv_s = jnp.stack([st[:, 1, :], up_s], axis=1)[None]
    tail0, (info0, ys0, dest0) = moe_experts(x1.reshape(n_p, d), x1s, 0)

    x2, k_p, v_p, kb, vt, qt, bias = _kvq_prompt(x1, info0, ys0, dest0, norm_kv[None, :],
                                                 norm_mix[1][None, :], w_kv, w_q[0], kn, qn)
    x2 = x2.reshape(n_p, d)
    x2s, _ = _combine(tail0[:half], tail0[half:], info0[n_p:], ys0, dest0[n_p * EXPERT_TOPK:], half)
    x2s = x2s[:bs]

    attn = _moba_prompt(qt, bias, kb, vt)
    x3 = _proj_res(x2, attn.reshape(n_p, d), w_o[0], 512)
    k_s, v_s, q_s = _kvq_sample(x2s, norm_kv[None, :], norm_mix[1][None, :], w_kv, w_q[0], kn, qn)
    heads = lambda z: z.reshape(bs, n_heads, HEAD_DIM)
    km, mb, lb, ob = _sample_blocks(heads(q_s), cache_k, cache_v, page_table)
    attn_s = _sample_combine(heads(q_s), heads(k_s), heads(v_s), km, mb, lb, ob)
    x3s = _proj_res(x2s, attn_s.reshape(bs, d), w_o[0], bs)

    tail1, (info1, ys1, dest1) = moe_experts(x3, x3s, 1)
    y_p, y_s = _combine(x3, tail1, info1, ys1, dest1, MOE_TILE)
    y_p = y_p.reshape(bp, tp, d)
    y_s = y_s[:bs].reshape(bs, 1, d)

    return (y_p, y_s, conv_p[None],
            k_p.reshape(bp, tp, n_heads, HEAD_DIM), v_p.reshape(bp, tp, n_heads, HEAD_DIM),
            conv_s,
            k_s.reshape(bs, 1, n_heads, HEAD_DIM), v_s.reshape(bs, 1, n_heads, HEAD_DIM))
```

```python
import functools

import jax
import jax.numpy as jnp
from jax import lax
from jax.experimental import pallas as pl
from jax.experimental.pallas import tpu as pltpu

F32 = jnp.float32
BF16 = jnp.bfloat16
I32 = jnp.int32

EPS = 1e-6
HEAD_DIM = 128
MOBA_BLOCK = 256
MOBA_TOPK = 3
N_GROUPS = 4
EXPERTS_PER_GROUP = 8
EXPERT_TOPK = 2
CONV_W = 3

LANE = 128
SUBLANE = 8
VMEM_LIMIT = 48 * 1024 * 1024
NEG = -1e30
BIG_IDX = 1 << 20
LOG2_E = 1.4426950408889634
QK_SCALE = HEAD_DIM ** -0.5
QK_SCALE_LOG2 = QK_SCALE * LOG2_E
V_ONES_ROWS = 16


def _cparams(*sem):
    return pltpu.CompilerParams(dimension_semantics=sem, vmem_limit_bytes=VMEM_LIMIT)


def _rms(x, g):
    ms = jnp.mean(x * x, axis=-1, keepdims=True)
    return x * lax.rsqrt(ms + EPS) * g


def _dot(a, b):
    return jnp.dot(a.astype(BF16), b.astype(BF16), preferred_element_type=F32)


def _mixer_prompt_kernel(x_ref, g_ref, win_ref, wc_ref, wout_ref, o_ref, st_ref, buf_ref, *, tm, d):
    @pl.when(pl.program_id(1) == 0)
    def _():
        buf_ref[0:SUBLANE, :] = jnp.zeros((SUBLANE, d), F32)

    x = x_ref[...]
    h = _rms(x, g_ref[...])
    u = _dot(h, win_ref[...])
    bg = u[:, :d]
    up = u[:, d:2 * d] * u[:, 2 * d:]
    buf_ref[SUBLANE:SUBLANE + tm, :] = up
    up1 = buf_ref[SUBLANE - 1:SUBLANE - 1 + tm, :]
    up2 = buf_ref[SUBLANE - 2:SUBLANE - 2 + tm, :]
    wc = wc_ref[...]
    y = wc[0:1] * up2 + wc[1:2] * up1 + wc[2:3] * up
    o_ref[...] = x + _dot(bg * y, wout_ref[...])
    last2 = buf_ref[tm + SUBLANE - 2:tm + SUBLANE, :]
    st_ref[...] = last2
    buf_ref[SUBLANE - 2:SUBLANE, :] = last2


def _mixer_prompt(x, g, w_in, w_c, w_out, tm=256):
    b, t, d = x.shape
    kern = functools.partial(_mixer_prompt_kernel, tm=tm, d=d)
    return pl.pallas_call(
        kern,
        grid=(b, t // tm),
        in_specs=[
            pl.BlockSpec((None, tm, d), lambda i, j: (i, j, 0)),
            pl.BlockSpec((1, d), lambda i, j: (0, 0)),
            pl.BlockSpec((d, 3 * d), lambda i, j: (0, 0)),
            pl.BlockSpec((CONV_W, d), lambda i, j: (0, 0)),
            pl.BlockSpec((d, d), lambda i, j: (0, 0)),
        ],
        out_specs=[
            pl.BlockSpec((None, tm, d), lambda i, j: (i, j, 0)),
            pl.BlockSpec((None, CONV_W - 1, d), lambda i, j: (i, 0, 0)),
        ],
        out_shape=[
            jax.ShapeDtypeStruct((b, t, d), F32),
            jax.ShapeDtypeStruct((b, CONV_W - 1, d), F32),
        ],
        scratch_shapes=[pltpu.VMEM((tm + SUBLANE, d), F32)],
        compiler_params=_cparams("arbitrary", "arbitrary"),
        name="mixer_prompt",
    )(x, g, w_in.astype(BF16), w_c, w_out.astype(BF16))


def _mixer_sample_kernel(x_ref, p0_ref, p1_ref, g_ref, win_ref, wc_ref, wout_ref, o_ref, up_ref, *, d):
    x = x_ref[...]
    h = _rms(x, g_ref[...])
    u = _dot(h, win_ref[...])
    bg = u[:, :d]
    up = u[:, d:2 * d] * u[:, 2 * d:]
    wc = wc_ref[...]
    y = wc[0:1] * p0_ref[...] + wc[1:2] * p1_ref[...] + wc[2:3] * up
    o_ref[...] = x + _dot(bg * y, wout_ref[...])
    up_ref[...] = up


def _mixer_sample(x, p0, p1, g, w_in, w_c, w_out):
    n, d = x.shape
    return pl.pallas_call(
        functools.partial(_mixer_sample_kernel, d=d),
        out_shape=[jax.ShapeDtypeStruct((n, d), F32), jax.ShapeDtypeStruct((n, d), F32)],
        compiler_params=pltpu.CompilerParams(vmem_limit_bytes=VMEM_LIMIT),
        name="mixer_sample",
    )(x, p0, p1, g, w_in, w_c, w_out)


def _token_tile(main_ref, tail_ref, n_main):
    return jnp.where(pl.program_id(0) < n_main, main_ref[...], tail_ref[...])


def _token_specs(tm, d, n_main):
    return [pl.BlockSpec((tm, d), lambda i, *_: (jnp.minimum(i, n_main - 1), 0)),
            pl.BlockSpec((tm, d), lambda i, *_: (jnp.maximum(i - n_main, 0), 0))]


def _router_kernel(xm_ref, xt_ref, g_ref, wr_ref, br_ref, info_ref, cnt_ref, run_ref, *, n_main):
    @pl.when(pl.program_id(0) == 0)
    def _():
        run_ref[...] = jnp.zeros(run_ref.shape, F32)

    xn = _rms(_token_tile(xm_ref, xt_ref, n_main), g_ref[...])
    lg = _dot(xn, wr_ref[...]) + br_ref[...]
    lane = lax.broadcasted_iota(I32, lg.shape, 1)
    ninf = jnp.float32(-jnp.inf)
    is_g = lane < N_GROUPS
    gl = jnp.where(is_g, lg, ninf)
    gmax = jnp.max(gl, axis=-1, keepdims=True)
    gsel = jnp.min(jnp.where(gl == gmax, lane, BIG_IDX), axis=-1, keepdims=True)
    gsum = jnp.sum(jnp.where(is_g, jnp.exp(gl - gmax), 0.0), axis=-1, keepdims=True)
    p_group = 1.0 / gsum
    lo = N_GROUPS + gsel * EXPERTS_PER_GROUP
    in_grp = (lane >= lo) & (lane < lo + EXPERTS_PER_GROUP)
    el = jnp.where(in_grp, lg, ninf)
    e1 = jnp.max(el, axis=-1, keepdims=True)
    i1 = jnp.min(jnp.where(el == e1, lane, BIG_IDX), axis=-1, keepdims=True)
    el2 = jnp.where(lane == i1, ninf, el)
    e2 = jnp.max(el2, axis=-1, keepdims=True)
    i2 = jnp.min(jnp.where(el2 == e2, lane, BIG_IDX), axis=-1, keepdims=True)
    r = jnp.exp(e2 - e1)
    w1 = p_group / (1.0 + r)
    w2 = w1 * r
    eid1 = i1 - N_GROUPS
    eid2 = i2 - N_GROUPS
    hit1 = lane == eid1
    hit2 = lane == eid2
    onehot = jnp.where(hit1 | hit2, 1.0, 0.0)
    tm = onehot.shape[0]
    tri = jnp.where(lax.broadcasted_iota(I32, (tm, tm), 0) > lax.broadcasted_iota(I32, (tm, tm), 1), 1.0, 0.0)
    before = _dot(tri, onehot) + run_ref[0:1, :]
    rank1 = jnp.sum(jnp.where(hit1, before, 0.0), axis=-1, keepdims=True)
    rank2 = jnp.sum(jnp.where(hit2, before, 0.0), axis=-1, keepdims=True)
    total = run_ref[0:1, :] + jnp.sum(onehot, axis=0, keepdims=True)
    run_ref[0:1, :] = total
    cnt_ref[...] = total
    cols = (eid1.astype(F32), eid2.astype(F32), w1, w2, rank1, rank2)
    info = jnp.zeros(lg.shape, F32)
    for c, val in enumerate(cols):
        info = jnp.where(lane == c, val, info)
    info_ref[...] = info


ROUTE_EID, ROUTE_W, ROUTE_RANK = 0, 2, 4
MOE_TILE = 512
MOE_ROUTER_TILE = 512
MOE_ROW_BLOCK = 512


def _router(x_main, x_tail, g, wr, br, tm):
    d = x_main.shape[1]
    n_main = x_main.shape[0] // tm
    n = x_main.shape[0] + x_tail.shape[0]
    return pl.pallas_call(
        functools.partial(_router_kernel, n_main=n_main),
        grid=(n // tm,),
        in_specs=_token_specs(tm, d, n_main) + [
            pl.BlockSpec((1, d), lambda i: (0, 0)),
            pl.BlockSpec((d, LANE), lambda i: (0, 0)),
            pl.BlockSpec((1, LANE), lambda i: (0, 0)),
        ],
        out_specs=[
            pl.BlockSpec((tm, LANE), lambda i: (i, 0)),
            pl.BlockSpec((1, LANE), lambda i: (0, 0)),
        ],
        out_shape=[jax.ShapeDtypeStruct((n, LANE), F32), jax.ShapeDtypeStruct((1, LANE), F32)],
        scratch_shapes=[pltpu.VMEM((SUBLANE, LANE), F32)],
        compiler_params=_cparams("arbitrary"),
        name="router",
    )(x_main, x_tail, g, wr, br)


ISSUE_UNROLL = 8


def _pack_bf16_pairs(x):
    c = x.shape[1] // 2
    bits = lax.bitcast_convert_type(x.astype(BF16).astype(F32), jnp.uint32)
    return (bits[:, :c] >> 16) | (bits[:, c:] & jnp.uint32(0xFFFF0000))


def _unpack_bf16_pairs(u):
    lo = lax.bitcast_convert_type(u << 16, F32)
    hi = lax.bitcast_convert_type(u & jnp.uint32(0xFFFF0000), F32)
    return jnp.concatenate([lo, hi], axis=1).astype(BF16)


def _row_copy(src, src_row, dst, dst_row, sem):
    return pltpu.make_async_copy(src.at[pl.ds(src_row, 1), :], dst.at[pl.ds(dst_row, 1), :], sem)


def _dispatch_kernel(dest_ref, xm_ref, xt_ref, g_ref, zeros_ref, xs_ref, buf, sem, *, tm, n_steps, n_main):
    del zeros_ref
    i = pl.program_id(0)
    slot = i % 2

    def wait_rows(s):
        for _ in range(EXPERT_TOPK):
            pltpu.make_async_copy(buf.at[s], xs_ref.at[pl.ds(0, tm), :], sem.at[s]).wait()

    @pl.when(i >= 2)
    def _():
        wait_rows(slot)

    buf[slot] = _pack_bf16_pairs(_rms(_token_tile(xm_ref, xt_ref, n_main), g_ref[...]))

    def issue(r, carry):
        a = (i * tm + r) * EXPERT_TOPK
        for k in range(EXPERT_TOPK):
            _row_copy(buf.at[slot], r, xs_ref, dest_ref[a + k], sem.at[slot]).start(priority=k % 2)
        return carry

    for r in range(tm):
        issue(r, 0)

    @pl.when(i == n_steps - 1)
    def _():
        if n_steps > 1:
            wait_rows(1 - slot)
        wait_rows(slot)


def _dispatch(x_main, x_tail, g, dest, p, tm):
    d = x_main.shape[1]
    n_main = x_main.shape[0] // tm
    n_steps = n_main + x_tail.shape[0] // tm
    grid_spec = pltpu.PrefetchScalarGridSpec(
        num_scalar_prefetch=1,
        grid=(n_steps,),
        in_specs=_token_specs(tm, d, n_main) + [
            pl.BlockSpec((1, d), lambda i, dst: (0, 0)),
            pl.BlockSpec(memory_space=pl.ANY),
        ],
        out_specs=pl.BlockSpec(memory_space=pl.ANY),
        scratch_shapes=[pltpu.VMEM((2, tm, d // 2), jnp.uint32), pltpu.SemaphoreType.DMA((2,))],
    )
    return pl.pallas_call(
        functools.partial(_dispatch_kernel, tm=tm, n_steps=n_steps, n_main=n_main),
        grid_spec=grid_spec,
        out_shape=jax.ShapeDtypeStruct((p, d // 2), jnp.uint32),
        input_output_aliases={4: 0},
        compiler_params=_cparams("arbitrary"),
        name="dispatch",
    )(dest, x_main, x_tail, g, jnp.zeros((p, d // 2), jnp.uint32))


def _combine_kernel(dest_ref, xm_ref, xt_ref, info_ref, ys_ref, om_ref, ot_ref, buf, sem, *,
                    tm, n_steps, n_main):
    i = pl.program_id(0)
    slot = i % 2

    def issue_step(step, s):
        def issue(r, carry):
            a = (step * tm + r) * EXPERT_TOPK
            for k in range(EXPERT_TOPK):
                _row_copy(ys_ref, dest_ref[a + k], buf.at[s, k], r, sem.at[s]).start(priority=k % 2)
            return carry
        for r in range(tm):
            issue(r, 0)

    @pl.when(i == 0)
    def _():
        issue_step(0, 0)

    @pl.when(i + 1 < n_steps)
    def _():
        issue_step(i + 1, 1 - slot)

    for k in range(EXPERT_TOPK):
        pltpu.make_async_copy(ys_ref.at[pl.ds(0, tm), :], buf.at[slot, k], sem.at[slot]).wait()
    out = _token_tile(xm_ref, xt_ref, n_main)
    for k in range(EXPERT_TOPK):
        out = out + info_ref[:, ROUTE_W + k:ROUTE_W + k + 1] * buf[slot, k]

    @pl.when(i < n_main)
    def _():
        om_ref[...] = out

    @pl.when(i >= n_main)
    def _():
        ot_ref[...] = out


def _combine(x_main, x_tail, info, ys, dest, tm):
    d = x_main.shape[1]
    n_main = x_main.shape[0] // tm
    n_steps = n_main + x_tail.shape[0] // tm
    grid_spec = pltpu.PrefetchScalarGridSpec(
        num_scalar_prefetch=1,
        grid=(n_steps,),
        in_specs=_token_specs(tm, d, n_main) + [
            pl.BlockSpec((tm, LANE), lambda i, dst: (i, 0)),
            pl.BlockSpec(memory_space=pl.ANY),
        ],
        out_specs=_token_specs(tm, d, n_main),
        scratch_shapes=[pltpu.VMEM((2, EXPERT_TOPK, tm, d), F32), pltpu.SemaphoreType.DMA((2,))],
    )
    return pl.pallas_call(
        functools.partial(_combine_kernel, tm=tm, n_steps=n_steps, n_main=n_main),
        grid_spec=grid_spec,
        out_shape=[jax.ShapeDtypeStruct(x_main.shape, F32), jax.ShapeDtypeStruct(x_tail.shape, F32)],
        compiler_params=_cparams("arbitrary"),
        name="combine",
    )(dest, x_main, x_tail, info, ys)


def _experts_kernel(be_ref, nu_ref, xs_ref, wg_ref, wu_ref, wd_ref, o_ref, wgb, wub, wdb):
    i = pl.program_id(0)
    e = be_ref[i]
    prev = be_ref[jnp.maximum(i - 1, 0)]

    @pl.when((i == 0) | (e != prev))
    def _():
        wgb[...] = wg_ref[...].astype(BF16)
        wub[...] = wu_ref[...].astype(BF16)
        wdb[...] = wd_ref[...].astype(BF16)

    @pl.when(i < nu_ref[0])
    def _():
        xb = _unpack_bf16_pairs(xs_ref[...])
        h1 = jnp.dot(xb, wgb[...], preferred_element_type=F32)
        h2 = jnp.dot(xb, wub[...], preferred_element_type=F32)
        h = h1 * jax.nn.sigmoid(h1) * h2
        o_ref[...] = jnp.dot(h.astype(BF16), wdb[...], preferred_element_type=F32)

    @pl.when(i >= nu_ref[0])
    def _():
        o_ref[...] = jnp.zeros(o_ref.shape, F32)


def _experts(xs, blk_expert, n_used, wg, wu, wd, layer, blk):
    p = xs.shape[0]
    d, de = wg.shape[-2:]
    n_blk = p // blk
    grid_spec = pltpu.PrefetchScalarGridSpec(
        num_scalar_prefetch=2,
        grid=(n_blk,),
        in_specs=[
            pl.BlockSpec((blk, d // 2), lambda i, be, nu: (jnp.minimum(i, nu[0] - 1), 0)),
            pl.BlockSpec((None, None, d, de), lambda i, be, nu: (layer, be[i], 0, 0)),
            pl.BlockSpec((None, None, d, de), lambda i, be, nu: (layer, be[i], 0, 0)),
            pl.BlockSpec((None, None, de, d), lambda i, be, nu: (layer, be[i], 0, 0)),
        ],
        out_specs=pl.BlockSpec((blk, d), lambda i, be, nu: (i, 0)),
        scratch_shapes=[pltpu.VMEM((d, de), BF16), pltpu.VMEM((d, de), BF16), pltpu.VMEM((de, d), BF16)],
    )
    return pl.pallas_call(
        _experts_kernel,
        grid_spec=grid_spec,
        out_shape=jax.ShapeDtypeStruct((p, d), F32),
        compiler_params=_cparams("arbitrary"),
        name="experts",
    )(blk_expert, n_used, xs, wg, wu, wd)


def _moe_experts(x_main, x_tail, layer, g, w_rg, b_rg, w_re, b_re, wg, wu, wd, *, tm_router, tm, blk):
    d = x_main.shape[1]
    n = x_main.shape[0] + x_tail.shape[0]
    n_exp = wg.shape[1]
    pad = LANE - N_GROUPS - n_exp
    wr = jnp.concatenate([w_rg, w_re, jnp.zeros((d, pad), F32)], axis=1)
    br = jnp.concatenate([b_rg, b_re, jnp.zeros((pad,), F32)])[None, :]
    info, cnt = _router(x_main, x_tail, g, wr, br, tm_router)

    a = n * EXPERT_TOPK
    experts = jnp.arange(n_exp, dtype=I32)
    counts = cnt[0, :n_exp].astype(I32)
    padded = ((counts + blk - 1) // blk) * blk
    pad_end = jnp.cumsum(padded)
    pad_start = pad_end - padded
    eid = info[:, ROUTE_EID:ROUTE_EID + EXPERT_TOPK].astype(I32)
    rank = info[:, ROUTE_RANK:ROUTE_RANK + EXPERT_TOPK].astype(I32)
    start = jnp.sum(jnp.where(eid[..., None] == experts, pad_start, 0), axis=-1)
    dest = (start + rank).reshape(a)
    n_blk = -(-a // blk) + n_exp
    n_used = (pad_end[-1] // blk).astype(I32)
    blk_ids = jnp.minimum(jnp.arange(n_blk, dtype=I32), n_used - 1)
    blk_expert = jnp.sum((blk_ids[:, None] * blk >= pad_end[None, :]).astype(I32), axis=1)
    blk_expert = jnp.minimum(blk_expert, n_exp - 1)
    xs = _dispatch(x_main, x_tail, g, dest, n_blk * blk, tm)
    ys = _experts(xs, blk_expert, n_used.reshape(1), wg, wu, wd, layer, blk)
    return info, ys, dest


def _head_rms(x, g):
    return x * lax.rsqrt(jnp.mean(x * x, axis=-1, keepdims=True) + EPS) * g


def _kvq_prompt_kernel(dest_ref, x_ref, info_ref, ys_ref, gkv_ref, gq_ref, wkv_ref, wq_ref, kn_ref, qn_ref,
                       x2_ref, k_ref, v_ref, kb_ref, vt_ref, qt_ref, bias_ref, km_ref, buf, sem,
                       *, n_heads, nb, n_steps):
    i = pl.program_id(1)
    lin = pl.program_id(0) * nb + i
    slot = lin % 2
    d = n_heads * HEAD_DIM
    tq = x_ref.shape[0]

    def issue_row(step, s, r):
        a = (step * tq + r) * EXPERT_TOPK
        for k in range(EXPERT_TOPK):
            _row_copy(ys_ref, dest_ref[a + k], buf.at[s, k], r, sem.at[s]).start(priority=k % 2)

    def wait_rows(s):
        for k in range(EXPERT_TOPK):
            pltpu.make_async_copy(ys_ref.at[pl.ds(0, tq), :], buf.at[s, k], sem.at[s]).wait()

    @pl.when(lin == 0)
    def _():
        lax.fori_loop(0, tq, lambda r, c: (issue_row(0, 0, r), c)[1], 0, unroll=ISSUE_UNROLL)

    @pl.when(i == 0)
    def _():
        km_ref[...] = jnp.zeros(km_ref.shape, F32)

    wait_rows(slot)
    x = x_ref[...]
    for k in range(EXPERT_TOPK):
        x = x + info_ref[:, ROUTE_W + k:ROUTE_W + k + 1] * buf[slot, k]
    x2_ref[...] = x
    nxt = jnp.minimum(lin + 1, n_steps - 1)
    rows_per_head = tq // n_heads

    kv = _dot(_rms(x, gkv_ref[...]), wkv_ref[...])
    q = _dot(_rms(x, gq_ref[...]), wq_ref[...])
    v_ref[...] = kv[:, d:]
    blk_iota = lax.broadcasted_iota(I32, (nb, tq), 0)
    row_iota = lax.broadcasted_iota(I32, (nb, HEAD_DIM), 0)
    ninf = jnp.float32(-jnp.inf)
    for h in range(n_heads):
        for r in range(h * rows_per_head, (h + 1) * rows_per_head):
            issue_row(nxt, 1 - slot, r)
        hs = slice(h * HEAD_DIM, (h + 1) * HEAD_DIM)
        kh = _head_rms(kv[:, hs], kn_ref[...])
        k_ref[:, hs] = kh
        kb_ref[h] = kh.astype(BF16)
        vt = kv[:, d + h * HEAD_DIM:d + (h + 1) * HEAD_DIM].T.astype(BF16)
        vt_ref[h] = jnp.concatenate([vt, jnp.ones((V_ONES_ROWS, tq), BF16)], axis=0)
        qt = _head_rms(q[:, hs], qn_ref[...]).T
        qt_ref[h] = (qt * QK_SCALE_LOG2).astype(BF16)
        kmh = km_ref[:, hs]
        gate = _dot(kmh, qt)
        gate = jnp.where(blk_iota < i, gate, ninf)
        sel = jnp.zeros((nb, tq), jnp.bool_)
        for _ in range(MOBA_TOPK):
            m = jnp.max(gate, axis=0, keepdims=True)
            idx = jnp.min(jnp.where(gate == m, blk_iota, BIG_IDX), axis=0, keepdims=True)
            hit = blk_iota == idx
            sel = sel | (hit & (m > ninf))
            gate = jnp.where(hit, ninf, gate)
        bias_ref[h] = jnp.where(sel, 0.0, NEG)
        kmean = jnp.mean(kh, axis=0, keepdims=True)
        km_ref[:, hs] = jnp.where(row_iota == i, kmean, kmh)

    @pl.when(lin == n_steps - 1)
    def _():
        wait_rows(1 - slot)


def _kvq_prompt(x, info, ys, dest, gkv, gq, w_kv, w_q, kn, qn):
    b, t, d = x.shape
    n_heads = d // HEAD_DIM
    tq = MOBA_BLOCK
    nb = t // tq
    kern = functools.partial(_kvq_prompt_kernel, n_heads=n_heads, nb=nb, n_steps=b * nb)
    const = lambda i, j, dst: (0, 0)
    tile = pl.BlockSpec((None, tq, d), lambda i, j, dst: (i, j, 0))
    head_t = lambda rows: pl.BlockSpec((None, n_heads, None, rows, tq), lambda i, j, dst: (i, 0, j, 0, 0))
    grid_spec = pltpu.PrefetchScalarGridSpec(
        num_scalar_prefetch=1,
        grid=(b, nb),
        in_specs=[
            tile,
            pl.BlockSpec((tq, LANE), lambda i, j, dst: (i * nb + j, 0)),
            pl.BlockSpec(memory_space=pl.ANY),
            pl.BlockSpec((1, d), const),
            pl.BlockSpec((1, d), const),
            pl.BlockSpec((d, 2 * d), const),
            pl.BlockSpec((d, d), const),
            pl.BlockSpec((1, HEAD_DIM), const),
            pl.BlockSpec((1, HEAD_DIM), const),
        ],
        out_specs=[
            tile,
            tile,
            tile,
            pl.BlockSpec((None, n_heads, tq, HEAD_DIM), lambda i, j, dst: (i, 0, j, 0)),
            head_t(HEAD_DIM + V_ONES_ROWS),
            head_t(HEAD_DIM),
            head_t(nb),
        ],
        scratch_shapes=[pltpu.VMEM((nb, d), F32), pltpu.VMEM((2, EXPERT_TOPK, tq, d), F32),
                        pltpu.SemaphoreType.DMA((2,))],
    )
    return pl.pallas_call(
        kern,
        grid_spec=grid_spec,
        out_shape=[
            jax.ShapeDtypeStruct((b, t, d), F32),
            jax.ShapeDtypeStruct((b, t, d), F32),
            jax.ShapeDtypeStruct((b, t, d), F32),
            jax.ShapeDtypeStruct((b, n_heads, t, HEAD_DIM), BF16),
            jax.ShapeDtypeStruct((b, n_heads, nb, HEAD_DIM + V_ONES_ROWS, tq), BF16),
            jax.ShapeDtypeStruct((b, n_heads, nb, HEAD_DIM, tq), BF16),
            jax.ShapeDtypeStruct((b, n_heads, nb, nb, tq), F32),
        ],
        compiler_params=_cparams("arbitrary", "arbitrary"),
        name="kvq_prompt",
    )(dest, x, info, ys, gkv, gq, w_kv.astype(BF16), w_q.astype(BF16), kn, qn)


def _moba_prompt_kernel(qt_ref, bias_ref, kb_ref, vt_ref, o_ref, *, tq, grp, heads):
    i = pl.program_id(2)
    qts = [qt_ref[h] for h in range(heads)]

    def scores(h, n):
        off = pl.multiple_of(n * tq, tq)
        return jnp.dot(kb_ref[h, pl.ds(off, tq), :], qts[h], preferred_element_type=F32)

    def update(h, s, n, m, acc):
        m_new = jnp.maximum(m, jnp.max(s, axis=0, keepdims=True).astype(F32))
        alpha = jnp.exp2(m - m_new)
        p = jnp.exp2(s - m_new.astype(BF16))
        acc = alpha * acc + jnp.dot(vt_ref[h, n], p, preferred_element_type=F32)
        return m_new, acc

    key_i = lax.broadcasted_iota(I32, (tq, tq), 0)
    qry_i = lax.broadcasted_iota(I32, (tq, tq), 1)
    def run_group(blocks, carries):
        s = [[(scores(h, n).astype(BF16) + bias(h).astype(BF16)) if bias is not None else
              jnp.where(key_i <= qry_i, scores(h, n), NEG).astype(BF16) for n, bias in blocks]
             for h in range(heads)]
        carries = list(carries)
        for c, (n, _) in enumerate(blocks):
            for h in range(heads):
                carries[h] = update(h, s[h][c], n, *carries[h])
        return tuple(carries)

    init = (jnp.full((1, tq), NEG, F32), jnp.zeros((HEAD_DIM + V_ONES_ROWS, tq), F32))
    first = [(i, None)] + [(c, functools.partial(lambda h, c: bias_ref[h, c:c + 1, :], c=c))
                           for c in range(grp - 1)]
    carries = run_group(first, (init,) * heads)

    def body(g, carries):
        base = (grp - 1) + g * grp
        blocks = [(base + c, functools.partial(lambda h, n: bias_ref[h, pl.ds(n, 1), :], n=base + c))
                  for c in range(grp)]
        return run_group(blocks, carries)

    n_groups = (jnp.maximum(i - (grp - 1), 0) + grp - 1) // grp
    carries = lax.fori_loop(0, n_groups, body, tuple(carries))
    for h in range(heads):
        _, acc = carries[h]
        out = acc[:HEAD_DIM] / acc[HEAD_DIM:HEAD_DIM + 1]
        o_ref[:, h * HEAD_DIM:(h + 1) * HEAD_DIM] = out.T.astype(o_ref.dtype)


def _moba_prompt(qt, bias, kb, vt, grp=4, heads=8):
    b, n_heads, nb, _, tq = qt.shape
    t = nb * tq
    assert nb % grp == 0
    assert n_heads % heads == 0
    return pl.pallas_call(
        functools.partial(_moba_prompt_kernel, tq=tq, grp=grp, heads=heads),
        grid=(b, n_heads // heads, nb),
        in_specs=[
            pl.BlockSpec((None, heads, None, HEAD_DIM, tq), lambda bi, h, i: (bi, h, i, 0, 0)),
            pl.BlockSpec((None, heads, None, nb, tq), lambda bi, h, i: (bi, h, i, 0, 0)),
            pl.BlockSpec((None, heads, t, HEAD_DIM), lambda bi, h, i: (bi, h, 0, 0),
                         pipeline_mode=pl.Buffered(1)),
            pl.BlockSpec((None, heads, nb, HEAD_DIM + V_ONES_ROWS, tq), lambda bi, h, i: (bi, h, 0, 0, 0),
                         pipeline_mode=pl.Buffered(1)),
        ],
        out_specs=pl.BlockSpec((None, tq, heads * HEAD_DIM), lambda bi, h, i: (bi, i, h)),
        out_shape=jax.ShapeDtypeStruct((b, t, n_heads * HEAD_DIM), BF16),
        compiler_params=_cparams("arbitrary", "arbitrary", "arbitrary"),
        name="moba_prompt",
    )(qt, bias, kb, vt)


def _proj_res_kernel(x_ref, a_ref, w_ref, o_ref):
    o_ref[...] = x_ref[...] + _dot(a_ref[...], w_ref[...])


def _proj_res(x, a, w, tm):
    n, d = x.shape
    w = w.astype(BF16)
    return pl.pallas_call(
        _proj_res_kernel,
        grid=(n // tm,),
        in_specs=[
            pl.BlockSpec((tm, d), lambda i: (i, 0)),
            pl.BlockSpec((tm, a.shape[1]), lambda i: (i, 0)),
            pl.BlockSpec(w.shape, lambda i: (0, 0)),
        ],
        out_specs=pl.BlockSpec((tm, d), lambda i: (i, 0)),
        out_shape=jax.ShapeDtypeStruct((n, d), F32),
        compiler_params=_cparams("arbitrary"),
        name="proj_res",
    )(x, a, w)


def _kvq_sample_kernel(x_ref, gkv_ref, gq_ref, wkv_ref, wq_ref, kn_ref, qn_ref, k_ref, v_ref, q_ref,
                       *, n_heads):
    d = n_heads * HEAD_DIM
    x = x_ref[...]
    kv = _dot(_rms(x, gkv_ref[...]), wkv_ref[...])
    q = _dot(_rms(x, gq_ref[...]), wq_ref[...])
    v_ref[...] = kv[:, d:]
    for h in range(n_heads):
        hs = slice(h * HEAD_DIM, (h + 1) * HEAD_DIM)
        k_ref[:, hs] = _head_rms(kv[:, hs], kn_ref[...])
        q_ref[:, hs] = _head_rms(q[:, hs], qn_ref[...])


def _kvq_sample(x, gkv, gq, w_kv, w_q, kn, qn):
    n, d = x.shape
    sds = jax.ShapeDtypeStruct((n, d), F32)
    return pl.pallas_call(
        functools.partial(_kvq_sample_kernel, n_heads=d // HEAD_DIM),
        out_shape=[sds, sds, sds],
        compiler_params=pltpu.CompilerParams(vmem_limit_bytes=VMEM_LIMIT),
        name="kvq_sample",
    )(x, gkv, gq, w_kv, w_q, kn, qn)


def _sample_blocks_kernel(pt_ref, q_ref, *refs, pages_per_step, pages_per_block, chunk):
    del pt_ref
    k_refs = refs[:pages_per_step]
    v_refs = refs[pages_per_step:2 * pages_per_step]
    km_ref, m_ref, l_ref, o_ref = refs[2 * pages_per_step:]
    q = q_ref[...] * QK_SCALE
    page_keys = k_refs[0].shape[0]
    n_blocks = pages_per_step // pages_per_block
    col = jnp.zeros((q.shape[0], 1), F32)
    state = [(col + NEG, col, jnp.zeros(q.shape, F32), jnp.zeros(q.shape, F32))] * n_blocks
    for r in range(pages_per_block):
        for t in range(page_keys // chunk):
            rows = slice(t * chunk, (t + 1) * chunk)
            for c in range(n_blocks):
                m, l, o, ksum = state[c]
                kc = k_refs[c * pages_per_block + r][rows]
                s = jnp.sum(kc * q[None], axis=-1, keepdims=True)
                m_new = jnp.maximum(m, s.max(axis=0))
                alpha = jnp.exp(m - m_new)
                p = jnp.exp(s - m_new[None])
                l = alpha * l + p.sum(axis=0)
                o = alpha * o + (p * v_refs[c * pages_per_block + r][rows]).sum(axis=0)
                state[c] = (m_new, l, o, ksum + kc.sum(axis=0))
    for c in range(n_blocks):
        m, l, o, ksum = state[c]
        km_ref[c] = ksum * (1.0 / (pages_per_block * page_keys))
        m_ref[c] = jnp.broadcast_to(m, q.shape)
        l_ref[c] = jnp.broadcast_to(l, q.shape)
        o_ref[c] = o


def _sample_blocks(q, cache_k, cache_v, page_table, pages_per_step=16, chunk=32):
    n_seq, n_pages = page_table.shape
    _, page, n_heads, hd = cache_k.shape
    ppb = MOBA_BLOCK // page
    assert MOBA_BLOCK % page == 0 and n_pages % pages_per_step == 0 and pages_per_step % ppb == 0
    steps = n_pages // pages_per_step
    bps = pages_per_step // ppb

    def page_spec(c):
        return pl.BlockSpec((None, page, n_heads, hd),
                            lambda s, g, pt: (pt[(s * steps + g) * pages_per_step + c], 0, 0, 0))

    blk_out = pl.BlockSpec((None, bps, n_heads, hd), lambda s, g, pt: (s, g, 0, 0))
    grid_spec = pltpu.PrefetchScalarGridSpec(
        num_scalar_prefetch=1,
        grid=(n_seq, steps),
        in_specs=[pl.BlockSpec((None, n_heads, hd), lambda s, g, pt: (s, 0, 0))]
        + [page_spec(c) for c in range(pages_per_step)] * 2,
        out_specs=[blk_out] * 4,
    )
    sds = jax.ShapeDtypeStruct((n_seq, n_pages // ppb, n_heads, hd), F32)
    return pl.pallas_call(
        functools.partial(_sample_blocks_kernel, pages_per_step=pages_per_step, pages_per_block=ppb,
                          chunk=chunk),
        grid_spec=grid_spec,
        out_shape=[sds] * 4,
        compiler_params=_cparams("arbitrary", "arbitrary"),
        name="sample_blocks",
    )(page_table.reshape(-1), q, *([cache_k] * pages_per_step), *([cache_v] * pages_per_step))


def _sample_combine_kernel(q_ref, kn_ref, vn_ref, km_ref, m_ref, l_ref, o_ref, out_ref):
    q = q_ref[...]
    qb = q.astype(BF16).astype(F32)
    gate = jnp.sum(km_ref[...].astype(BF16).astype(F32) * qb[None], axis=-1, keepdims=True)
    blk_iota = lax.broadcasted_iota(I32, gate.shape, 0)
    ninf = jnp.float32(-jnp.inf)
    sel = jnp.zeros(gate.shape, jnp.bool_)
    for _ in range(MOBA_TOPK):
        g_max = jnp.max(gate, axis=0, keepdims=True)
        idx = jnp.min(jnp.where(gate == g_max, blk_iota, BIG_IDX), axis=0, keepdims=True)
        hit = blk_iota == idx
        sel = sel | (hit & (g_max > ninf))
        gate = jnp.where(hit, ninf, gate)
    s_new = jnp.sum(q * QK_SCALE * kn_ref[...], axis=-1, keepdims=True)
    mb = m_ref[...]
    m_tot = jnp.maximum(jnp.max(jnp.where(sel, mb, ninf), axis=0), s_new)
    w = jnp.where(sel, jnp.exp(mb - m_tot[None]), 0.0)
    p_new = jnp.exp(s_new - m_tot)
    l = jnp.sum(w * l_ref[...], axis=0) + p_new
    o = jnp.sum(w * o_ref[...], axis=0) + p_new * vn_ref[...]
    out_ref[...] = o / l


def _sample_combine(q, k_new, v_new, km, mb, lb, ob):
    n_seq, nblk, n_heads, hd = km.shape
    row = pl.BlockSpec((None, n_heads, hd), lambda s: (s, 0, 0))
    blk = pl.BlockSpec((None, nblk, n_heads, hd), lambda s: (s, 0, 0, 0))
    return pl.pallas_call(
        _sample_combine_kernel,
        grid=(n_seq,),
        in_specs=[row, row, row, blk, blk, blk, blk],
        out_specs=row,
        out_shape=jax.ShapeDtypeStruct((n_seq, n_heads, hd), F32),
        compiler_params=_cparams("arbitrary"),
        name="sample_combine",
    )(q, k_new, v_new, km, mb, lb, ob)


def kernel(x_prompt, x_sample, state_conv, cache_k, cache_v, page_table, norm_mix, norm_ffn, norm_kv,
           w_conv_in, w_conv, w_conv_out, w_kv, k_norm, w_q, q_norm, w_o, w_router_group,
           b_router_group, w_router_expert, b_router_expert, w_expert_gate, w_expert_up, w_expert_down):
    bp, tp, d = x_prompt.shape
    bs = x_sample.shape[0]
    n_heads = d // HEAD_DIM
    kn = k_norm[None, :]
    qn = q_norm[0][None, :]

    n_p = bp * tp
    half = MOE_ROUTER_TILE // 2

    def moe_experts(x_main, x_sample_rows, layer):
        tail = jnp.concatenate([x_sample_rows, jnp.zeros((MOE_ROUTER_TILE - bs, d), F32)], axis=0)
        return tail, _moe_experts(x_main, tail, layer, norm_ffn[layer][None, :],
                                  w_router_group[layer], b_router_group[layer],
                                  w_router_expert[layer], b_router_expert[layer],
                                  w_expert_gate, w_expert_up, w_expert_down,
                                  tm_router=MOE_ROUTER_TILE, tm=MOE_TILE, blk=MOE_ROW_BLOCK)

    x1, conv_p = _mixer_prompt(x_prompt, norm_mix[0][None, :], w_conv_in[0], w_conv[0], w_conv_out[0])
    st = state_conv[0]
    x1s, up_s = _mixer_sample(x_sample.reshape(bs, d), st[:, 0, :], st[:, 1, :], norm_mix[0][None, :],
                              w_conv_in[0], w_conv[0], w_conv_out[0])
    conv_s = jnp.stack([st[:, 1, :], up_s], axis=1)[None]
    tail0, (info0, ys0, dest0) = moe_experts(x1.reshape(n_p, d), x1s, 0)

    x2, k_p, v_p, kb, vt, qt, bias = _kvq_prompt(x1, info0, ys0, dest0, norm_kv[None, :],
                                                 norm_mix[1][None, :], w_kv, w_q[0], kn, qn)
    x2 = x2.reshape(n_p, d)
    x2s, _ = _combine(tail0[:half], tail0[half:], info0[n_p:], ys0, dest0[n_p * EXPERT_TOPK:], half)
    x2s = x2s[:bs]

    attn = _moba_prompt(qt, bias, kb, vt)
    x3 = _proj_res(x2, attn.reshape(n_p, d), w_o[0], 512)
    k_s, v_s, q_s = _kvq_sample(x2s, norm_kv[None, :], norm_mix[1][None, :], w_kv, w_q[0], kn, qn)
    heads = lambda z: z.reshape(bs, n_heads, HEAD_DIM)
    km, mb, lb, ob = _sample_blocks(heads(q_s), cache_k, cache_v, page_table)
    attn_s = _sample_combine(heads(q_s), heads(k_s), heads(v_s), km, mb, lb, ob)
    x3s = _proj_res(x2s, attn_s.reshape(bs, d), w_o[0], bs)

    tail1, (info1, ys1, dest1) = moe_experts(x3, x3s, 1)
    y_p, y_s = _combine(x3, tail1, info1, ys1, dest1, MOE_TILE)
    y_p = y_p.reshape(bp, tp, d)
    y_s = y_s[:bs].reshape(bs, 1, d)

    return (y_p, y_s, conv_p[None],
            k_p.reshape(bp, tp, n_heads, HEAD_DIM), v_p.reshape(bp, tp, n_heads, HEAD_DIM),
            conv_s,
            k_s.reshape(bs, 1, n_heads, HEAD_DIM), v_s.reshape(bs, 1, n_heads, HEAD_DIM))
```

```python
import functools

import jax
import jax.numpy as jnp
from jax import lax
from jax.experimental import pallas as pl
from jax.experimental.pallas import tpu as pltpu

F32 = jnp.float32
BF16 = jnp.bfloat16
I32 = jnp.int32

EPS = 1e-6
HEAD_DIM = 128
MOBA_BLOCK = 256
MOBA_TOPK = 3
N_GROUPS = 4
EXPERTS_PER_GROUP = 8
EXPERT_TOPK = 2
CONV_W = 3

LANE = 128
SUBLANE = 8
VMEM_LIMIT = 48 * 1024 * 1024
NEG = -1e30
BIG_IDX = 1 << 20
LOG2_E = 1.4426950408889634
QK_SCALE = HEAD_DIM ** -0.5
QK_SCALE_LOG2 = QK_SCALE * LOG2_E
V_ONES_ROWS = 16


def _cparams(*sem):
    return pltpu.CompilerParams(dimension_semantics=sem, vmem_limit_bytes=VMEM_LIMIT)


def _rms(x, g):
    ms = jnp.mean(x * x, axis=-1, keepdims=True)
    return x * lax.rsqrt(ms + EPS) * g


def _dot(a, b):
    return jnp.dot(a.astype(BF16), b.astype(BF16), preferred_element_type=F32)


def _mixer_prompt_kernel(x_ref, g_ref, win_ref, wc_ref, wout_ref, o_ref, st_ref, buf_ref, *, tm, d):
    @pl.when(pl.program_id(1) == 0)
    def _():
        buf_ref[0:SUBLANE, :] = jnp.zeros((SUBLANE, d), F32)

    x = x_ref[...]
    h = _rms(x, g_ref[...])
    u = _dot(h, win_ref[...])
    bg = u[:, :d]
    up = u[:, d:2 * d] * u[:, 2 * d:]
    buf_ref[SUBLANE:SUBLANE + tm, :] = up
    up1 = buf_ref[SUBLANE - 1:SUBLANE - 1 + tm, :]
    up2 = buf_ref[SUBLANE - 2:SUBLANE - 2 + tm, :]
    wc = wc_ref[...]
    y = wc[0:1] * up2 + wc[1:2] * up1 + wc[2:3] * up
    o_ref[...] = x + _dot(bg * y, wout_ref[...])
    last2 = buf_ref[tm + SUBLANE - 2:tm + SUBLANE, :]
    st_ref[...] = last2
    buf_ref[SUBLANE - 2:SUBLANE, :] = last2


def _mixer_prompt(x, g, w_in, w_c, w_out, tm=512):
    b, t, d = x.shape
    kern = functools.partial(_mixer_prompt_kernel, tm=tm, d=d)
    return pl.pallas_call(
        kern,
        grid=(b, t // tm),
        in_specs=[
            pl.BlockSpec((None, tm, d), lambda i, j: (i, j, 0)),
            pl.BlockSpec((1, d), lambda i, j: (0, 0)),
            pl.BlockSpec((d, 3 * d), lambda i, j: (0, 0)),
            pl.BlockSpec((CONV_W, d), lambda i, j: (0, 0)),
            pl.BlockSpec((d, d), lambda i, j: (0, 0)),
        ],
        out_specs=[
            pl.BlockSpec((None, tm, d), lambda i, j: (i, j, 0)),
            pl.BlockSpec((None, CONV_W - 1, d), lambda i, j: (i, 0, 0)),
        ],
        out_shape=[
            jax.ShapeDtypeStruct((b, t, d), F32),
            jax.ShapeDtypeStruct((b, CONV_W - 1, d), F32),
        ],
        scratch_shapes=[pltpu.VMEM((tm + SUBLANE, d), F32)],
        compiler_params=_cparams("arbitrary", "arbitrary"),
        name="mixer_prompt",
    )(x, g, w_in.astype(BF16), w_c, w_out.astype(BF16))


def _mixer_sample_kernel(x_ref, p0_ref, p1_ref, g_ref, win_ref, wc_ref, wout_ref, o_ref, up_ref, *, d):
    x = x_ref[...]
    h = _rms(x, g_ref[...])
    u = _dot(h, win_ref[...])
    bg = u[:, :d]
    up = u[:, d:2 * d] * u[:, 2 * d:]
    wc = wc_ref[...]
    y = wc[0:1] * p0_ref[...] + wc[1:2] * p1_ref[...] + wc[2:3] * up
    o_ref[...] = x + _dot(bg * y, wout_ref[...])
    up_ref[...] = up


def _mixer_sample(x, p0, p1, g, w_in, w_c, w_out):
    n, d = x.shape
    return pl.pallas_call(
        functools.partial(_mixer_sample_kernel, d=d),
        out_shape=[jax.ShapeDtypeStruct((n, d), F32), jax.ShapeDtypeStruct((n, d), F32)],
        compiler_params=pltpu.CompilerParams(vmem_limit_bytes=VMEM_LIMIT),
        name="mixer_sample",
    )(x, p0, p1, g, w_in, w_c, w_out)


def _token_tile(main_ref, tail_ref, n_main):
    return jnp.where(pl.program_id(0) < n_main, main_ref[...], tail_ref[...])


def _token_specs(tm, d, n_main):
    return [pl.BlockSpec((tm, d), lambda i, *_: (jnp.minimum(i, n_main - 1), 0)),
            pl.BlockSpec((tm, d), lambda i, *_: (jnp.maximum(i - n_main, 0), 0))]


def _router_kernel(xm_ref, xt_ref, g_ref, wr_ref, br_ref, info_ref, cnt_ref, run_ref, *, n_main):
    @pl.when(pl.program_id(0) == 0)
    def _():
        run_ref[...] = jnp.zeros(run_ref.shape, F32)

    xn = _rms(_token_tile(xm_ref, xt_ref, n_main), g_ref[...])
    lg = _dot(xn, wr_ref[...]) + br_ref[...]
    lane = lax.broadcasted_iota(I32, lg.shape, 1)
    ninf = jnp.float32(-jnp.inf)
    is_g = lane < N_GROUPS
    gl = jnp.where(is_g, lg, ninf)
    gmax = jnp.max(gl, axis=-1, keepdims=True)
    gsel = jnp.min(jnp.where(gl == gmax, lane, BIG_IDX), axis=-1, keepdims=True)
    gsum = jnp.sum(jnp.where(is_g, jnp.exp(gl - gmax), 0.0), axis=-1, keepdims=True)
    p_group = 1.0 / gsum
    lo = N_GROUPS + gsel * EXPERTS_PER_GROUP
    in_grp = (lane >= lo) & (lane < lo + EXPERTS_PER_GROUP)
    el = jnp.where(in_grp, lg, ninf)
    e1 = jnp.max(el, axis=-1, keepdims=True)
    i1 = jnp.min(jnp.where(el == e1, lane, BIG_IDX), axis=-1, keepdims=True)
    el2 = jnp.where(lane == i1, ninf, el)
    e2 = jnp.max(el2, axis=-1, keepdims=True)
    i2 = jnp.min(jnp.where(el2 == e2, lane, BIG_IDX), axis=-1, keepdims=True)
    r = jnp.exp(e2 - e1)
    w1 = p_group / (1.0 + r)
    w2 = w1 * r
    eid1 = i1 - N_GROUPS
    eid2 = i2 - N_GROUPS
    hit1 = lane == eid1
    hit2 = lane == eid2
    onehot = jnp.where(hit1 | hit2, 1.0, 0.0)
    tm = onehot.shape[0]
    tri = jnp.where(lax.broadcasted_iota(I32, (tm, tm), 0) > lax.broadcasted_iota(I32, (tm, tm), 1), 1.0, 0.0)
    before = _dot(tri, onehot) + run_ref[0:1, :]
    rank1 = jnp.sum(jnp.where(hit1, before, 0.0), axis=-1, keepdims=True)
    rank2 = jnp.sum(jnp.where(hit2, before, 0.0), axis=-1, keepdims=True)
    total = run_ref[0:1, :] + jnp.sum(onehot, axis=0, keepdims=True)
    run_ref[0:1, :] = total
    cnt_ref[...] = total
    cols = (eid1.astype(F32), eid2.astype(F32), w1, w2, rank1, rank2)
    info = jnp.zeros(lg.shape, F32)
    for c, val in enumerate(cols):
        info = jnp.where(lane == c, val, info)
    info_ref[...] = info


ROUTE_EID, ROUTE_W, ROUTE_RANK = 0, 2, 4
MOE_TILE = 512
MOE_ROUTER_TILE = 512
MOE_ROW_BLOCK = 512


def _router(x_main, x_tail, g, wr, br, tm):
    d = x_main.shape[1]
    n_main = x_main.shape[0] // tm
    n = x_main.shape[0] + x_tail.shape[0]
    return pl.pallas_call(
        functools.partial(_router_kernel, n_main=n_main),
        grid=(n // tm,),
        in_specs=_token_specs(tm, d, n_main) + [
            pl.BlockSpec((1, d), lambda i: (0, 0)),
            pl.BlockSpec((d, LANE), lambda i: (0, 0)),
            pl.BlockSpec((1, LANE), lambda i: (0, 0)),
        ],
        out_specs=[
            pl.BlockSpec((tm, LANE), lambda i: (i, 0)),
            pl.BlockSpec((1, LANE), lambda i: (0, 0)),
        ],
        out_shape=[jax.ShapeDtypeStruct((n, LANE), F32), jax.ShapeDtypeStruct((1, LANE), F32)],
        scratch_shapes=[pltpu.VMEM((SUBLANE, LANE), F32)],
        compiler_params=_cparams("arbitrary"),
        name="router",
    )(x_main, x_tail, g, wr, br)


ISSUE_UNROLL = 8


def _pack_bf16_pairs(x):
    c = x.shape[1] // 2
    bits = lax.bitcast_convert_type(x.astype(BF16).astype(F32), jnp.uint32)
    return (bits[:, :c] >> 16) | (bits[:, c:] & jnp.uint32(0xFFFF0000))


def _unpack_bf16_pairs(u):
    lo = lax.bitcast_convert_type(u << 16, F32)
    hi = lax.bitcast_convert_type(u & jnp.uint32(0xFFFF0000), F32)
    return jnp.concatenate([lo, hi], axis=1).astype(BF16)


def _row_copy(src, src_row, dst, dst_row, sem):
    return pltpu.make_async_copy(src.at[pl.ds(src_row, 1), :], dst.at[pl.ds(dst_row, 1), :], sem)


def _dispatch_kernel(dest_ref, xm_ref, xt_ref, g_ref, zeros_ref, xs_ref, buf, sem, *, tm, n_steps, n_main):
    del zeros_ref
    i = pl.program_id(0)
    slot = i % 2

    def wait_rows(s):
        for _ in range(EXPERT_TOPK):
            pltpu.make_async_copy(buf.at[s], xs_ref.at[pl.ds(0, tm), :], sem.at[s]).wait()

    @pl.when(i >= 2)
    def _():
        wait_rows(slot)

    buf[slot] = _pack_bf16_pairs(_rms(_token_tile(xm_ref, xt_ref, n_main), g_ref[...]))

    def issue(r, carry):
        a = (i * tm + r) * EXPERT_TOPK
        for k in range(EXPERT_TOPK):
            _row_copy(buf.at[slot], r, xs_ref, dest_ref[a + k], sem.at[slot]).start(priority=k % 2)
        return carry

    for r in range(tm):
        issue(r, 0)

    @pl.when(i == n_steps - 1)
    def _():
        if n_steps > 1:
            wait_rows(1 - slot)
        wait_rows(slot)


def _dispatch(x_main, x_tail, g, dest, p, tm):
    d = x_main.shape[1]
    n_main = x_main.shape[0] // tm
    n_steps = n_main + x_tail.shape[0] // tm
    grid_spec = pltpu.PrefetchScalarGridSpec(
        num_scalar_prefetch=1,
        grid=(n_steps,),
        in_specs=_token_specs(tm, d, n_main) + [
            pl.BlockSpec((1, d), lambda i, dst: (0, 0)),
            pl.BlockSpec(memory_space=pl.ANY),
        ],
        out_specs=pl.BlockSpec(memory_space=pl.ANY),
        scratch_shapes=[pltpu.VMEM((2, tm, d // 2), jnp.uint32), pltpu.SemaphoreType.DMA((2,))],
    )
    return pl.pallas_call(
        functools.partial(_dispatch_kernel, tm=tm, n_steps=n_steps, n_main=n_main),
        grid_spec=grid_spec,
        out_shape=jax.ShapeDtypeStruct((p, d // 2), jnp.uint32),
        input_output_aliases={4: 0},
        compiler_params=_cparams("arbitrary"),
        name="dispatch",
    )(dest, x_main, x_tail, g, jnp.zeros((p, d // 2), jnp.uint32))


def _combine_kernel(dest_ref, xm_ref, xt_ref, info_ref, ys_ref, om_ref, ot_ref, buf, sem, *,
                    tm, n_steps, n_main):
    i = pl.program_id(0)
    slot = i % 2

    def issue_step(step, s):
        def issue(r, carry):
            a = (step * tm + r) * EXPERT_TOPK
            for k in range(EXPERT_TOPK):
                _row_copy(ys_ref, dest_ref[a + k], buf.at[s, k], r, sem.at[s]).start(priority=k % 2)
            return carry
        for r in range(tm):
            issue(r, 0)

    @pl.when(i == 0)
    def _():
        issue_step(0, 0)

    @pl.when(i + 1 < n_steps)
    def _():
        issue_step(i + 1, 1 - slot)

    for k in range(EXPERT_TOPK):
        pltpu.make_async_copy(ys_ref.at[pl.ds(0, tm), :], buf.at[slot, k], sem.at[slot]).wait()
    out = _token_tile(xm_ref, xt_ref, n_main)
    for k in range(EXPERT_TOPK):
        out = out + info_ref[:, ROUTE_W + k:ROUTE_W + k + 1] * buf[slot, k]

    @pl.when(i < n_main)
    def _():
        om_ref[...] = out

    @pl.when(i >= n_main)
    def _():
        ot_ref[...] = out


def _combine(x_main, x_tail, info, ys, dest, tm):
    d = x_main.shape[1]
    n_main = x_main.shape[0] // tm
    n_steps = n_main + x_tail.shape[0] // tm
    grid_spec = pltpu.PrefetchScalarGridSpec(
        num_scalar_prefetch=1,
        grid=(n_steps,),
        in_specs=_token_specs(tm, d, n_main) + [
            pl.BlockSpec((tm, LANE), lambda i, dst: (i, 0)),
            pl.BlockSpec(memory_space=pl.ANY),
        ],
        out_specs=_token_specs(tm, d, n_main),
        scratch_shapes=[pltpu.VMEM((2, EXPERT_TOPK, tm, d), F32), pltpu.SemaphoreType.DMA((2,))],
    )
    return pl.pallas_call(
        functools.partial(_combine_kernel, tm=tm, n_steps=n_steps, n_main=n_main),
        grid_spec=grid_spec,
        out_shape=[jax.ShapeDtypeStruct(x_main.shape, F32), jax.ShapeDtypeStruct(x_tail.shape, F32)],
        compiler_params=_cparams("arbitrary"),
        name="combine",
    )(dest, x_main, x_tail, info, ys)


def _experts_kernel(be_ref, nu_ref, xs_ref, wg_ref, wu_ref, wd_ref, o_ref, wgb, wub, wdb):
    i = pl.program_id(0)
    e = be_ref[i]
    prev = be_ref[jnp.maximum(i - 1, 0)]

    @pl.when((i == 0) | (e != prev))
    def _():
        wgb[...] = wg_ref[...].astype(BF16)
        wub[...] = wu_ref[...].astype(BF16)
        wdb[...] = wd_ref[...].astype(BF16)

    @pl.when(i < nu_ref[0])
    def _():
        xb = _unpack_bf16_pairs(xs_ref[...])
        h1 = jnp.dot(xb, wgb[...], preferred_element_type=F32)
        h2 = jnp.dot(xb, wub[...], preferred_element_type=F32)
        h = h1 * jax.nn.sigmoid(h1) * h2
        o_ref[...] = jnp.dot(h.astype(BF16), wdb[...], preferred_element_type=F32)

    @pl.when(i >= nu_ref[0])
    def _():
        o_ref[...] = jnp.zeros(o_ref.shape, F32)


def _experts(xs, blk_expert, n_used, wg, wu, wd, layer, blk):
    p = xs.shape[0]
    d, de = wg.shape[-2:]
    n_blk = p // blk
    grid_spec = pltpu.PrefetchScalarGridSpec(
        num_scalar_prefetch=2,
        grid=(n_blk,),
        in_specs=[
            pl.BlockSpec((blk, d // 2), lambda i, be, nu: (jnp.minimum(i, nu[0] - 1), 0)),
            pl.BlockSpec((None, None, d, de), lambda i, be, nu: (layer, be[i], 0, 0)),
            pl.BlockSpec((None, None, d, de), lambda i, be, nu: (layer, be[i], 0, 0)),
            pl.BlockSpec((None, None, de, d), lambda i, be, nu: (layer, be[i], 0, 0)),
        ],
        out_specs=pl.BlockSpec((blk, d), lambda i, be, nu: (i, 0)),
        scratch_shapes=[pltpu.VMEM((d, de), BF16), pltpu.VMEM((d, de), BF16), pltpu.VMEM((de, d), BF16)],
    )
    return pl.pallas_call(
        _experts_kernel,
        grid_spec=grid_spec,
        out_shape=jax.ShapeDtypeStruct((p, d), F32),
        compiler_params=_cparams("arbitrary"),
        name="experts",
    )(blk_expert, n_used, xs, wg, wu, wd)


def _moe_experts(x_main, x_tail, layer, g, w_rg, b_rg, w_re, b_re, wg, wu, wd, *, tm_router, tm, blk):
    d = x_main.shape[1]
    n = x_main.shape[0] + x_tail.shape[0]
    n_exp = wg.shape[1]
    pad = LANE - N_GROUPS - n_exp
    wr = jnp.concatenate([w_rg, w_re, jnp.zeros((d, pad), F32)], axis=1)
    br = jnp.concatenate([b_rg, b_re, jnp.zeros((pad,), F32)])[None, :]
    info, cnt = _router(x_main, x_tail, g, wr, br, tm_router)

    a = n * EXPERT_TOPK
    experts = jnp.arange(n_exp, dtype=I32)
    counts = cnt[0, :n_exp].astype(I32)
    padded = ((counts + blk - 1) // blk) * blk
    pad_end = jnp.cumsum(padded)
    pad_start = pad_end - padded
    eid = info[:, ROUTE_EID:ROUTE_EID + EXPERT_TOPK].astype(I32)
    rank = info[:, ROUTE_RANK:ROUTE_RANK + EXPERT_TOPK].astype(I32)
    start = jnp.sum(jnp.where(eid[..., None] == experts, pad_start, 0), axis=-1)
    dest = (start + rank).reshape(a)
    n_blk = -(-a // blk) + n_exp
    n_used = (pad_end[-1] // blk).astype(I32)
    blk_ids = jnp.minimum(jnp.arange(n_blk, dtype=I32), n_used - 1)
    blk_expert = jnp.sum((blk_ids[:, None] * blk >= pad_end[None, :]).astype(I32), axis=1)
    blk_expert = jnp.minimum(blk_expert, n_exp - 1)
    xs = _dispatch(x_main, x_tail, g, dest, n_blk * blk, tm)
    ys = _experts(xs, blk_expert, n_used.reshape(1), wg, wu, wd, layer, blk)
    return info, ys, dest


def _head_rms(x, g):
    return x * lax.rsqrt(jnp.mean(x * x, axis=-1, keepdims=True) + EPS) * g


def _kvq_prompt_kernel(dest_ref, x_ref, info_ref, ys_ref, gkv_ref, gq_ref, wkv_ref, wq_ref, kn_ref, qn_ref,
                       x2_ref, k_ref, v_ref, kb_ref, vt_ref, qt_ref, bias_ref, km_ref, buf, sem,
                       *, n_heads, nb, n_steps):
    i = pl.program_id(1)
    lin = pl.program_id(0) * nb + i
    slot = lin % 2
    d = n_heads * HEAD_DIM
    tq = x_ref.shape[0]

    def issue_row(step, s, r):
        a = (step * tq + r) * EXPERT_TOPK
        for k in range(EXPERT_TOPK):
            _row_copy(ys_ref, dest_ref[a + k], buf.at[s, k], r, sem.at[s]).start(priority=k % 2)

    def wait_rows(s):
        for k in range(EXPERT_TOPK):
            pltpu.make_async_copy(ys_ref.at[pl.ds(0, tq), :], buf.at[s, k], sem.at[s]).wait()

    @pl.when(lin == 0)
    def _():
        lax.fori_loop(0, tq, lambda r, c: (issue_row(0, 0, r), c)[1], 0, unroll=ISSUE_UNROLL)

    @pl.when(i == 0)
    def _():
        km_ref[...] = jnp.zeros(km_ref.shape, F32)

    wait_rows(slot)
    x = x_ref[...]
    for k in range(EXPERT_TOPK):
        x = x + info_ref[:, ROUTE_W + k:ROUTE_W + k + 1] * buf[slot, k]
    x2_ref[...] = x
    nxt = jnp.minimum(lin + 1, n_steps - 1)
    rows_per_head = tq // n_heads

    kv = _dot(_rms(x, gkv_ref[...]), wkv_ref[...])
    q = _dot(_rms(x, gq_ref[...]), wq_ref[...])
    v_ref[...] = kv[:, d:]
    blk_iota = lax.broadcasted_iota(I32, (nb, tq), 0)
    row_iota = lax.broadcasted_iota(I32, (nb, HEAD_DIM), 0)
    ninf = jnp.float32(-jnp.inf)
    for h in range(n_heads):
        for r in range(h * rows_per_head, (h + 1) * rows_per_head):
            issue_row(nxt, 1 - slot, r)
        hs = slice(h * HEAD_DIM, (h + 1) * HEAD_DIM)
        kh = _head_rms(kv[:, hs], kn_ref[...])
        k_ref[:, hs] = kh
        kb_ref[h] = kh.astype(BF16)
        vt = kv[:, d + h * HEAD_DIM:d + (h + 1) * HEAD_DIM].T.astype(BF16)
        vt_ref[h] = jnp.concatenate([vt, jnp.ones((V_ONES_ROWS, tq), BF16)], axis=0)
        qt = _head_rms(q[:, hs], qn_ref[...]).T
        qt_ref[h] = (qt * QK_SCALE_LOG2).astype(BF16)
        kmh = km_ref[:, hs]
        gate = _dot(kmh, qt)
        gate = jnp.where(blk_iota < i, gate, ninf)
        sel = jnp.zeros((nb, tq), jnp.bool_)
        for _ in range(MOBA_TOPK):
            m = jnp.max(gate, axis=0, keepdims=True)
            idx = jnp.min(jnp.where(gate == m, blk_iota, BIG_IDX), axis=0, keepdims=True)
            hit = blk_iota == idx
            sel = sel | (hit & (m > ninf))
            gate = jnp.where(hit, ninf, gate)
        bias_ref[h] = jnp.where(sel, 0.0, NEG)
        kmean = jnp.mean(kh, axis=0, keepdims=True)
        km_ref[:, hs] = jnp.where(row_iota == i, kmean, kmh)

    @pl.when(lin == n_steps - 1)
    def _():
        wait_rows(1 - slot)


def _kvq_prompt(x, info, ys, dest, gkv, gq, w_kv, w_q, kn, qn):
    b, t, d = x.shape
    n_heads = d // HEAD_DIM
    tq = MOBA_BLOCK
    nb = t // tq
    kern = functools.partial(_kvq_prompt_kernel, n_heads=n_heads, nb=nb, n_steps=b * nb)
    const = lambda i, j, dst: (0, 0)
    tile = pl.BlockSpec((None, tq, d), lambda i, j, dst: (i, j, 0))
    head_t = lambda rows: pl.BlockSpec((None, n_heads, None, rows, tq), lambda i, j, dst: (i, 0, j, 0, 0))
    grid_spec = pltpu.PrefetchScalarGridSpec(
        num_scalar_prefetch=1,
        grid=(b, nb),
        in_specs=[
            tile,
            pl.BlockSpec((tq, LANE), lambda i, j, dst: (i * nb + j, 0)),
            pl.BlockSpec(memory_space=pl.ANY),
            pl.BlockSpec((1, d), const),
            pl.BlockSpec((1, d), const),
            pl.BlockSpec((d, 2 * d), const),
            pl.BlockSpec((d, d), const),
            pl.BlockSpec((1, HEAD_DIM), const),
            pl.BlockSpec((1, HEAD_DIM), const),
        ],
        out_specs=[
            tile,
            tile,
            tile,
            pl.BlockSpec((None, n_heads, tq, HEAD_DIM), lambda i, j, dst: (i, 0, j, 0)),
            head_t(HEAD_DIM + V_ONES_ROWS),
            head_t(HEAD_DIM),
            head_t(nb),
        ],
        scratch_shapes=[pltpu.VMEM((nb, d), F32), pltpu.VMEM((2, EXPERT_TOPK, tq, d), F32),
                        pltpu.SemaphoreType.DMA((2,))],
    )
    return pl.pallas_call(
        kern,
        grid_spec=grid_spec,
        out_shape=[
            jax.ShapeDtypeStruct((b, t, d), F32),
            jax.ShapeDtypeStruct((b, t, d), F32),
            jax.ShapeDtypeStruct((b, t, d), F32),
            jax.ShapeDtypeStruct((b, n_heads, t, HEAD_DIM), BF16),
            jax.ShapeDtypeStruct((b, n_heads, nb, HEAD_DIM + V_ONES_ROWS, tq), BF16),
            jax.ShapeDtypeStruct((b, n_heads, nb, HEAD_DIM, tq), BF16),
            jax.ShapeDtypeStruct((b, n_heads, nb, nb, tq), F32),
        ],
        compiler_params=_cparams("arbitrary", "arbitrary"),
        name="kvq_prompt",
    )(dest, x, info, ys, gkv, gq, w_kv.astype(BF16), w_q.astype(BF16), kn, qn)


def _moba_prompt_kernel(qt_ref, bias_ref, kb_ref, vt_ref, o_ref, *, tq, grp, heads):
    i = pl.program_id(2)
    qts = [qt_ref[h] for h in range(heads)]

    def scores(h, n):
        off = pl.multiple_of(n * tq, tq)
        return jnp.dot(kb_ref[h, pl.ds(off, tq), :], qts[h], preferred_element_type=F32)

    def update(h, s, n, m, acc):
        m_new = jnp.maximum(m, jnp.max(s, axis=0, keepdims=True).astype(F32))
        alpha = jnp.exp2(m - m_new)
        p = jnp.exp2(s - m_new.astype(BF16))
        acc = alpha * acc + jnp.dot(vt_ref[h, n], p, preferred_element_type=F32)
        return m_new, acc

    key_i = lax.broadcasted_iota(I32, (tq, tq), 0)
    qry_i = lax.broadcasted_iota(I32, (tq, tq), 1)
    def run_group(blocks, carries):
        s = [[(scores(h, n).astype(BF16) + bias(h).astype(BF16)) if bias is not None else
              jnp.where(key_i <= qry_i, scores(h, n), NEG).astype(BF16) for n, bias in blocks]
             for h in range(heads)]
        carries = list(carries)
        for c, (n, _) in enumerate(blocks):
            for h in range(heads):
                carries[h] = update(h, s[h][c], n, *carries[h])
        return tuple(carries)

    init = (jnp.full((1, tq), NEG, F32), jnp.zeros((HEAD_DIM + V_ONES_ROWS, tq), F32))
    first = [(i, None)] + [(c, functools.partial(lambda h, c: bias_ref[h, c:c + 1, :], c=c))
                           for c in range(grp - 1)]
    carries = run_group(first, (init,) * heads)

    def body(g, carries):
        base = (grp - 1) + g * grp
        blocks = [(base + c, functools.partial(lambda h, n: bias_ref[h, pl.ds(n, 1), :], n=base + c))
                  for c in range(grp)]
        return run_group(blocks, carries)

    n_groups = (jnp.maximum(i - (grp - 1), 0) + grp - 1) // grp
    carries = lax.fori_loop(0, n_groups, body, tuple(carries))
    for h in range(heads):
        _, acc = carries[h]
        out = acc[:HEAD_DIM] / acc[HEAD_DIM:HEAD_DIM + 1]
        o_ref[:, h * HEAD_DIM:(h + 1) * HEAD_DIM] = out.T.astype(o_ref.dtype)


def _moba_prompt(qt, bias, kb, vt, grp=4, heads=8):
    b, n_heads, nb, _, tq = qt.shape
    t = nb * tq
    assert nb % grp == 0
    assert n_heads % heads == 0
    return pl.pallas_call(
        functools.partial(_moba_prompt_kernel, tq=tq, grp=grp, heads=heads),
        grid=(b, n_heads // heads, nb),
        in_specs=[
            pl.BlockSpec((None, heads, None, HEAD_DIM, tq), lambda bi, h, i: (bi, h, i, 0, 0)),
            pl.BlockSpec((None, heads, None, nb, tq), lambda bi, h, i: (bi, h, i, 0, 0)),
            pl.BlockSpec((None, heads, t, HEAD_DIM), lambda bi, h, i: (bi, h, 0, 0),
                         pipeline_mode=pl.Buffered(1)),
            pl.BlockSpec((None, heads, nb, HEAD_DIM + V_ONES_ROWS, tq), lambda bi, h, i: (bi, h, 0, 0, 0),
                         pipeline_mode=pl.Buffered(1)),
        ],
        out_specs=pl.BlockSpec((None, tq, heads * HEAD_DIM), lambda bi, h, i: (bi, i, h)),
        out_shape=jax.ShapeDtypeStruct((b, t, n_heads * HEAD_DIM), BF16),
        compiler_params=_cparams("arbitrary", "arbitrary", "arbitrary"),
        name="moba_prompt",
    )(qt, bias, kb, vt)


def _proj_res_kernel(x_ref, a_ref, w_ref, o_ref):
    o_ref[...] = x_ref[...] + _dot(a_ref[...], w_ref[...])


def _proj_res(x, a, w, tm):
    n, d = x.shape
    w = w.astype(BF16)
    return pl.pallas_call(
        _proj_res_kernel,
        grid=(n // tm,),
        in_specs=[
            pl.BlockSpec((tm, d), lambda i: (i, 0)),
            pl.BlockSpec((tm, a.shape[1]), lambda i: (i, 0)),
            pl.BlockSpec(w.shape, lambda i: (0, 0)),
        ],
        out_specs=pl.BlockSpec((tm, d), lambda i: (i, 0)),
        out_shape=jax.ShapeDtypeStruct((n, d), F32),
        compiler_params=_cparams("arbitrary"),
        name="proj_res",
    )(x, a, w)


def _kvq_sample_kernel(x_ref, gkv_ref, gq_ref, wkv_ref, wq_ref, kn_ref, qn_ref, k_ref, v_ref, q_ref,
                       *, n_heads):
    d = n_heads * HEAD_DIM
    x = x_ref[...]
    kv = _dot(_rms(x, gkv_ref[...]), wkv_ref[...])
    q = _dot(_rms(x, gq_ref[...]), wq_ref[...])
    v_ref[...] = kv[:, d:]
    for h in range(n_heads):
        hs = slice(h * HEAD_DIM, (h + 1) * HEAD_DIM)
        k_ref[:, hs] = _head_rms(kv[:, hs], kn_ref[...])
        q_ref[:, hs] = _head_rms(q[:, hs], qn_ref[...])


def _kvq_sample(x, gkv, gq, w_kv, w_q, kn, qn):
    n, d = x.shape
    sds = jax.ShapeDtypeStruct((n, d), F32)
    return pl.pallas_call(
        functools.partial(_kvq_sample_kernel, n_heads=d // HEAD_DIM),
        out_shape=[sds, sds, sds],
        compiler_params=pltpu.CompilerParams(vmem_limit_bytes=VMEM_LIMIT),
        name="kvq_sample",
    )(x, gkv, gq, w_kv, w_q, kn, qn)


def _sample_blocks_kernel(pt_ref, q_ref, *refs, pages_per_step, pages_per_block, chunk):
    del pt_ref
    k_refs = refs[:pages_per_step]
    v_refs = refs[pages_per_step:2 * pages_per_step]
    km_ref, m_ref, l_ref, o_ref = refs[2 * pages_per_step:]
    q = q_ref[...] * QK_SCALE
    page_keys = k_refs[0].shape[0]
    n_blocks = pages_per_step // pages_per_block
    col = jnp.zeros((q.shape[0], 1), F32)
    state = [(col + NEG, col, jnp.zeros(q.shape, F32), jnp.zeros(q.shape, F32))] * n_blocks
    for r in range(pages_per_block):
        for t in range(page_keys // chunk):
            rows = slice(t * chunk, (t + 1) * chunk)
            for c in range(n_blocks):
                m, l, o, ksum = state[c]
                kc = k_refs[c * pages_per_block + r][rows]
                s = jnp.sum(kc * q[None], axis=-1, keepdims=True)
                m_new = jnp.maximum(m, s.max(axis=0))
                alpha = jnp.exp(m - m_new)
                p = jnp.exp(s - m_new[None])
                l = alpha * l + p.sum(axis=0)
                o = alpha * o + (p * v_refs[c * pages_per_block + r][rows]).sum(axis=0)
                state[c] = (m_new, l, o, ksum + kc.sum(axis=0))
    for c in range(n_blocks):
        m, l, o, ksum = state[c]
        km_ref[c] = ksum * (1.0 / (pages_per_block * page_keys))
        m_ref[c] = jnp.broadcast_to(m, q.shape)
        l_ref[c] = jnp.broadcast_to(l, q.shape)
        o_ref[c] = o


def _sample_blocks(q, cache_k, cache_v, page_table, pages_per_step=16, chunk=32):
    n_seq, n_pages = page_table.shape
    _, page, n_heads, hd = cache_k.shape
    ppb = MOBA_BLOCK // page
    assert MOBA_BLOCK % page == 0 and n_pages % pages_per_step == 0 and pages_per_step % ppb == 0
    steps = n_pages // pages_per_step
    bps = pages_per_step // ppb

    def page_spec(c):
        return pl.BlockSpec((None, page, n_heads, hd),
                            lambda s, g, pt: (pt[(s * steps + g) * pages_per_step + c], 0, 0, 0))

    blk_out = pl.BlockSpec((None, bps, n_heads, hd), lambda s, g, pt: (s, g, 0, 0))
    grid_spec = pltpu.PrefetchScalarGridSpec(
        num_scalar_prefetch=1,
        grid=(n_seq, steps),
        in_specs=[pl.BlockSpec((None, n_heads, hd), lambda s, g, pt: (s, 0, 0))]
        + [page_spec(c) for c in range(pages_per_step)] * 2,
        out_specs=[blk_out] * 4,
    )
    sds = jax.ShapeDtypeStruct((n_seq, n_pages // ppb, n_heads, hd), F32)
    return pl.pallas_call(
        functools.partial(_sample_blocks_kernel, pages_per_step=pages_per_step, pages_per_block=ppb,
                          chunk=chunk),
        grid_spec=grid_spec,
        out_shape=[sds] * 4,
        compiler_params=_cparams("arbitrary", "arbitrary"),
        name="sample_blocks",
    )(page_table.reshape(-1), q, *([cache_k] * pages_per_step), *([cache_v] * pages_per_step))


def _sample_combine_kernel(q_ref, kn_ref, vn_ref, km_ref, m_ref, l_ref, o_ref, out_ref):
    q = q_ref[...]
    qb = q.astype(BF16).astype(F32)
    gate = jnp.sum(km_ref[...].astype(BF16).astype(F32) * qb[None], axis=-1, keepdims=True)
    blk_iota = lax.broadcasted_iota(I32, gate.shape, 0)
    ninf = jnp.float32(-jnp.inf)
    sel = jnp.zeros(gate.shape, jnp.bool_)
    for _ in range(MOBA_TOPK):
        g_max = jnp.max(gate, axis=0, keepdims=True)
        idx = jnp.min(jnp.where(gate == g_max, blk_iota, BIG_IDX), axis=0, keepdims=True)
        hit = blk_iota == idx
        sel = sel | (hit & (g_max > ninf))
        gate = jnp.where(hit, ninf, gate)
    s_new = jnp.sum(q * QK_SCALE * kn_ref[...], axis=-1, keepdims=True)
    mb = m_ref[...]
    m_tot = jnp.maximum(jnp.max(jnp.where(sel, mb, ninf), axis=0), s_new)
    w = jnp.where(sel, jnp.exp(mb - m_tot[None]), 0.0)
    p_new = jnp.exp(s_new - m_tot)
    l = jnp.sum(w * l_ref[...], axis=0) + p_new
    o = jnp.sum(w * o_ref[...], axis=0) + p_new * vn_ref[...]
    out_ref[...] = o / l


def _sample_combine(q, k_new, v_new, km, mb, lb, ob):
    n_seq, nblk, n_heads, hd = km.shape
    row = pl.BlockSpec((None, n_heads, hd), lambda s: (s, 0, 0))
    blk = pl.BlockSpec((None, nblk, n_heads, hd), lambda s: (s, 0, 0, 0))
    return pl.pallas_call(
        _sample_combine_kernel,
        grid=(n_seq,),
        in_specs=[row, row, row, blk, blk, blk, blk],
        out_specs=row,
        out_shape=jax.ShapeDtypeStruct((n_seq, n_heads, hd), F32),
        compiler_params=_cparams("arbitrary"),
        name="sample_combine",
    )(q, k_new, v_new, km, mb, lb, ob)


def kernel(x_prompt, x_sample, state_conv, cache_k, cache_v, page_table, norm_mix, norm_ffn, norm_kv,
           w_conv_in, w_conv, w_conv_out, w_kv, k_norm, w_q, q_norm, w_o, w_router_group,
           b_router_group, w_router_expert, b_router_expert, w_expert_gate, w_expert_up, w_expert_down):
    bp, tp, d = x_prompt.shape
    bs = x_sample.shape[0]
    n_heads = d // HEAD_DIM
    kn = k_norm[None, :]
    qn = q_norm[0][None, :]

    n_p = bp * tp
    half = MOE_ROUTER_TILE // 2

    def moe_experts(x_main, x_sample_rows, layer):
        tail = jnp.concatenate([x_sample_rows, jnp.zeros((MOE_ROUTER_TILE - bs, d), F32)], axis=0)
        return tail, _moe_experts(x_main, tail, layer, norm_ffn[layer][None, :],
                                  w_router_group[layer], b_router_group[layer],
                                  w_router_expert[layer], b_router_expert[layer],
                                  w_expert_gate, w_expert_up, w_expert_down,
                                  tm_router=MOE_ROUTER_TILE, tm=MOE_TILE, blk=MOE_ROW_BLOCK)

    x1, conv_p = _mixer_prompt(x_prompt, norm_mix[0][None, :], w_conv_in[0], w_conv[0], w_conv_out[0])
    st = state_conv[0]
    x1s, up_s = _mixer_sample(x_sample.reshape(bs, d), st[:, 0, :], st[:, 1, :], norm_mix[0][None, :],
                              w_conv_in[0], w_conv[0], w_conv_out[0])
    conv_s = jnp.stack([st[:, 1, :], up_s], axis=1)[None]
    tail0, (info0, ys0, dest0) = moe_experts(x1.reshape(n_p, d), x1s, 0)

    x2, k_p, v_p, kb, vt, qt, bias = _kvq_prompt(x1, info0, ys0, dest0, norm_kv[None, :],
                                                 norm_mix[1][None, :], w_kv, w_q[0], kn, qn)
    x2 = x2.reshape(n_p, d)
    x2s, _ = _combine(tail0[:half], tail0[half:], info0[n_p:], ys0, dest0[n_p * EXPERT_TOPK:], half)
    x2s = x2s[:bs]

    attn = _moba_prompt(qt, bias, kb, vt)
    x3 = _proj_res(x2, attn.reshape(n_p, d), w_o[0], 512)
    k_s, v_s, q_s = _kvq_sample(x2s, norm_kv[None, :], norm_mix[1][None, :], w_kv, w_q[0], kn, qn)
    heads = lambda z: z.reshape(bs, n_heads, HEAD_DIM)
    km, mb, lb, ob = _sample_blocks(heads(q_s), cache_k, cache_v, page_table)
    attn_s = _sample_combine(heads(q_s), heads(k_s), heads(v_s), km, mb, lb, ob)
    x3s = _proj_res(x2s, attn_s.reshape(bs, d), w_o[0], bs)

    tail1, (info1, ys1, dest1) = moe_experts(x3, x3s, 1)
    y_p, y_s = _combine(x3, tail1, info1, ys1, dest1, MOE_TILE)
    y_p = y_p.reshape(bp, tp, d)
    y_s = y_s[:bs].reshape(bs, 1, d)

    return (y_p, y_s, conv_p[None],
            k_p.reshape(bp, tp, n_heads, HEAD_DIM), v_p.reshape(bp, tp, n_heads, HEAD_DIM),
            conv_s,
            k_s.reshape(bs, 1, n_heads, HEAD_DIM), v_s.reshape(bs, 1, n_heads, HEAD_DIM))
```
